```python
import math
import jax, jax.numpy as jnp
from jax import lax
import numpy as np

D_MODEL = 1024
BATCH = 8
SEQ = 2048
DEPTH = 2
DEC_BATCH = 128
DEC_SEQ = 8
PAST_LEN = 2048
PAGE_SIZE = 128

N_A_LAYERS = DEPTH // 2
N_B_LAYERS = DEPTH - N_A_LAYERS
SSM_GROUP = 16
N_GROUPS = D_MODEL // SSM_GROUP
STATE_DIM = 64
DT_MIN = 1e-3
DT_MAX = 1e-1
N_HEADS = 16
HEAD_DIM = D_MODEL // N_HEADS
ROT_DIM = HEAD_DIM // 4
ROPE_THETA = 500000.0
MOBA_BLOCK = 256
MOBA_TOPK = 3
Q_CHUNK = 128
D_FF = 4 * D_MODEL
EPS = 1e-6
NEG_INF = -1e30

kernel_name = 'yoco_s5_moba_step'


def rmsnorm(x, g):
    xf = x.astype(jnp.float32)
    y = xf * lax.rsqrt(jnp.mean(xf * xf, axis=-1, keepdims=True) + EPS)
    return (y * g.astype(jnp.float32)).astype(x.dtype)


def partial_rope(x, pos):
    half = ROT_DIM // 2
    inv = ROPE_THETA ** (-jnp.arange(half, dtype=jnp.float32) / half)
    ang = pos.astype(jnp.float32)[:, None] * inv[None, :]
    cos = jnp.cos(ang)[None, :, None, :]
    sin = jnp.sin(ang)[None, :, None, :]
    xr = x[..., :ROT_DIM].astype(jnp.float32)
    x1, x2 = xr[..., :half], xr[..., half:]
    rot = jnp.concatenate([x1 * cos - x2 * sin, x2 * cos + x1 * sin], axis=-1)
    return jnp.concatenate([rot.astype(x.dtype), x[..., ROT_DIM:]], axis=-1)


def s5_ssm(u, h0_re, h0_im, lam_re, lam_im, log_dt, b_re, b_im, c_re, c_im, d_skip):
    f32 = jnp.float32
    Bt, T, _ = u.shape
    lr, li = lam_re.astype(f32), lam_im.astype(f32)
    dt = jnp.exp(log_dt.astype(f32))[:, None]
    mag = jnp.exp(lr * dt)
    ab_re, ab_im = mag * jnp.cos(li * dt), mag * jnp.sin(li * dt)
    nr, ni = ab_re - 1.0, ab_im
    den = lr * lr + li * li
    f_re = (nr * lr + ni * li) / den
    f_im = (ni * lr - nr * li) / den
    br, bi = b_re.astype(f32), b_im.astype(f32)
    bb_re = f_re[..., None] * br - f_im[..., None] * bi
    bb_im = f_re[..., None] * bi + f_im[..., None] * br
    ug = u.astype(f32).reshape(Bt, T, N_GROUPS, SSM_GROUP)
    bu_re = jnp.einsum('btgc,gpc->tbgp', ug, bb_re)
    bu_im = jnp.einsum('btgc,gpc->tbgp', ug, bb_im)
    if h0_re is not None:
        hr, hi = h0_re.astype(f32), h0_im.astype(f32)
        bu_re = bu_re.at[0].add(ab_re * hr - ab_im * hi)
        bu_im = bu_im.at[0].add(ab_re * hi + ab_im * hr)
    a_re = jnp.broadcast_to(ab_re, (T, 1, N_GROUPS, STATE_DIM))
    a_im = jnp.broadcast_to(ab_im, (T, 1, N_GROUPS, STATE_DIM))

    def combine(e1, e2):
        a1r, a1i, b1r, b1i = e1
        a2r, a2i, b2r, b2i = e2
        return (a1r * a2r - a1i * a2i, a1r * a2i + a1i * a2r,
                a2r * b1r - a2i * b1i + b2r, a2r * b1i + a2i * b1r + b2i)

    _, _, xs_re, xs_im = lax.associative_scan(combine, (a_re, a_im, bu_re, bu_im), axis=0)
    y = (jnp.einsum('gcp,tbgp->btgc', c_re.astype(f32), xs_re)
         - jnp.einsum('gcp,tbgp->btgc', c_im.astype(f32), xs_im))
    y = y.reshape(Bt, T, D_MODEL) + d_skip.astype(f32) * u.astype(f32)
    return y, xs_re[-1], xs_im[-1]


def shared_kv(h, pos, kv_norm, w_kv, k_norm):
    Bt, T, _ = h.shape
    kv = rmsnorm(h, kv_norm) @ w_kv
    k = kv[..., :N_HEADS * HEAD_DIM].reshape(Bt, T, N_HEADS, HEAD_DIM)
    v = kv[..., N_HEADS * HEAD_DIM:].reshape(Bt, T, N_HEADS, HEAD_DIM)
    k = partial_rope(rmsnorm(k, k_norm), pos)
    return k, v


def moba_attend_seq(q, qpos, k, v):
    L = k.shape[0]
    nb = -(-L // MOBA_BLOCK)
    pad = nb * MOBA_BLOCK - L
    k = jnp.pad(k, ((0, pad), (0, 0), (0, 0)))
    v = jnp.pad(v, ((0, pad), (0, 0), (0, 0)))
    kb = k.reshape(nb, MOBA_BLOCK, N_HEADS, HEAD_DIM).transpose(2, 0, 1, 3)
    vb = v.reshape(nb, MOBA_BLOCK, N_HEADS, HEAD_DIM).transpose(2, 0, 1, 3)
    kmean = jnp.mean(kb.astype(jnp.float32), axis=2)
    n_sel = min(MOBA_TOPK, nb)
    head_idx = jnp.arange(N_HEADS)[None, :, None]
    scale = HEAD_DIM ** -0.5

    def attend(args):
        qc, pc = args
        c = qc.shape[0]
        qf = qc.astype(jnp.float32)
        own = pc // MOBA_BLOCK
        gate = jnp.einsum('chd,hnd->chn', qf, kmean)
        is_past = jnp.arange(nb)[None, None, :] < own[:, None, None]
        gate = jnp.where(is_past, gate, NEG_INF)
        _, sel = lax.top_k(gate, n_sel)
        blocks = jnp.concatenate(
            [jnp.broadcast_to(own[:, None, None], (c, N_HEADS, 1)), sel], axis=-1)
        slot_ok = jnp.concatenate(
            [jnp.ones((c, N_HEADS, 1), dtype=bool), sel < own[:, None, None]], axis=-1)
        kg = kb[head_idx, blocks].astype(jnp.float32)
        vg = vb[head_idx, blocks].astype(jnp.float32)
        kpos = blocks[..., None] * MOBA_BLOCK + jnp.arange(MOBA_BLOCK)
        valid = slot_ok[..., None] & (kpos <= pc[:, None, None, None])
        s = jnp.einsum('chd,chsnd->chsn', qf, kg) * scale
        s = jnp.where(valid, s, NEG_INF)
        p = jax.nn.softmax(s.reshape(c, N_HEADS, -1), axis=-1).reshape(s.shape)
        o = jnp.einsum('chsn,chsnd->chd', p, vg)
        return o.astype(qc.dtype)

    T = q.shape[0]
    if T > Q_CHUNK and T % Q_CHUNK == 0:
        o = lax.map(attend, (q.reshape(T // Q_CHUNK, Q_CHUNK, N_HEADS, HEAD_DIM),
                             qpos.reshape(T // Q_CHUNK, Q_CHUNK)))
        return o.reshape(T, N_HEADS, HEAD_DIM)
    return attend((q, qpos))


def run_trunk(x, pos, h0_re, h0_im, attend, p):
    h = x
    fin_re, fin_im = [], []
    k_sh, v_sh = None, None
    Bt, T, _ = x.shape
    for layer in range(DEPTH):
        if layer < N_A_LAYERS:
            a = layer
            u = rmsnorm(h, p['ssm_norm'][a])
            y, sr, si = s5_ssm(u,
                               None if h0_re is None else h0_re[a],
                               None if h0_im is None else h0_im[a],
                               p['ssm_lambda_re'][a], p['ssm_lambda_im'][a], p['ssm_log_dt'][a],
                               p['ssm_b_re'][a], p['ssm_b_im'][a], p['ssm_c_re'][a], p['ssm_c_im'][a],
                               p['ssm_d'][a])
            y = jax.nn.gelu(y).astype(h.dtype)
            z = y @ p['ssm_w_glu'][a]
            h = h + z[..., :D_MODEL] * jax.nn.sigmoid(z[..., D_MODEL:])
            fin_re.append(sr)
            fin_im.append(si)
        else:
            b = layer - N_A_LAYERS
            if k_sh is None:
                k_sh, v_sh = shared_kv(h, pos, p['kv_norm'], p['w_kv'], p['k_norm'])
            u = rmsnorm(h, p['attn_norm'][b])
            q = (u @ p['w_q'][b]).reshape(Bt, T, N_HEADS, HEAD_DIM)
            q = partial_rope(rmsnorm(q, p['q_norm'][b]), pos)
            o = attend(q, k_sh, v_sh)
            h = h + o.reshape(Bt, T, N_HEADS * HEAD_DIM) @ p['w_o'][b]
        u = rmsnorm(h, p['mlp_norm'][layer])
        h = h + jnp.square(jax.nn.relu(u @ p['w_up'][layer])) @ p['w_down'][layer]
    return h, k_sh, v_sh, jnp.stack(fin_re), jnp.stack(fin_im)


def setup_inputs(seed: int = 0) -> dict:
    key = jax.random.key(seed)
    ks = jax.random.split(key, 32)
    f32 = jnp.float32
    n_pages = PAST_LEN // PAGE_SIZE
    n_used = DEC_BATCH * n_pages
    n_pool = n_used + n_used // 4
    G, P, C = N_GROUPS, STATE_DIM, SSM_GROUP
    HD = N_HEADS * HEAD_DIM

    def nrm(k, shape, scale):
        return jax.random.normal(k, shape, f32) * scale

    def gain(k, shape):
        return 1.0 + 0.02 * jax.random.normal(k, shape, f32)

    n_idx = jnp.arange(P, dtype=f32)
    page_table = jax.random.permutation(ks[6], n_pool)[:n_used].reshape(DEC_BATCH, n_pages).astype(jnp.int32)
    return {
        'x_prompt': nrm(ks[0], (BATCH, SEQ, D_MODEL), 1.0),
        'x_sample': nrm(ks[1], (DEC_BATCH, DEC_SEQ, D_MODEL), 1.0),
        'state_ssm_re': nrm(ks[2], (N_A_LAYERS, DEC_BATCH, G, P), 0.1),
        'state_ssm_im': nrm(ks[3], (N_A_LAYERS, DEC_BATCH, G, P), 0.1),
        'cache_k': nrm(ks[4], (n_pool, PAGE_SIZE, N_HEADS, HEAD_DIM), 1.0),
        'cache_v': nrm(ks[5], (n_pool, PAGE_SIZE, N_HEADS, HEAD_DIM), 1.0),
        'page_table': page_table,
        'ssm_norm': gain(ks[7], (N_A_LAYERS, D_MODEL)),
        'ssm_lambda_re': -0.5 + 0.01 * jax.random.normal(ks[8], (N_A_LAYERS, G, P), f32),
        'ssm_lambda_im': math.pi * n_idx + 0.01 * jax.random.normal(ks[9], (N_A_LAYERS, G, P), f32),
        'ssm_log_dt': jax.random.uniform(ks[10], (N_A_LAYERS, G), f32, math.log(DT_MIN), math.log(DT_MAX)),
        'ssm_b_re': nrm(ks[11], (N_A_LAYERS, G, P, C), (2 * C) ** -0.5),
        'ssm_b_im': nrm(ks[12], (N_A_LAYERS, G, P, C), (2 * C) ** -0.5),
        'ssm_c_re': nrm(ks[13], (N_A_LAYERS, G, C, P), P ** -0.5),
        'ssm_c_im': nrm(ks[14], (N_A_LAYERS, G, C, P), P ** -0.5),
        'ssm_d': nrm(ks[15], (N_A_LAYERS, D_MODEL), 1.0),
        'ssm_w_glu': nrm(ks[16], (N_A_LAYERS, D_MODEL, 2 * D_MODEL), D_MODEL ** -0.5),
        'kv_norm': gain(ks[17], (D_MODEL,)),
        'w_kv': nrm(ks[18], (D_MODEL, 2 * HD), D_MODEL ** -0.5),
        'k_norm': gain(ks[19], (HEAD_DIM,)),
        'attn_norm': gain(ks[20], (N_B_LAYERS, D_MODEL)),
        'w_q': nrm(ks[21], (N_B_LAYERS, D_MODEL, HD), D_MODEL ** -0.5),
        'q_norm': gain(ks[22], (N_B_LAYERS, HEAD_DIM)),
        'w_o': nrm(ks[23], (N_B_LAYERS, HD, D_MODEL), HD ** -0.5),
        'mlp_norm': gain(ks[24], (DEPTH, D_MODEL)),
        'w_up': nrm(ks[25], (DEPTH, D_MODEL, D_FF), D_MODEL ** -0.5),
        'w_down': nrm(ks[26], (DEPTH, D_FF, D_MODEL), D_FF ** -0.5),
    }


def reference(x_prompt, x_sample, state_ssm_re, state_ssm_im, cache_k, cache_v, page_table,
              ssm_norm, ssm_lambda_re, ssm_lambda_im, ssm_log_dt, ssm_b_re, ssm_b_im,
              ssm_c_re, ssm_c_im, ssm_d, ssm_w_glu, kv_norm, w_kv, k_norm,
              attn_norm, w_q, q_norm, w_o, mlp_norm, w_up, w_down):
    p = dict(ssm_norm=ssm_norm, ssm_lambda_re=ssm_lambda_re, ssm_lambda_im=ssm_lambda_im,
             ssm_log_dt=ssm_log_dt, ssm_b_re=ssm_b_re, ssm_b_im=ssm_b_im,
             ssm_c_re=ssm_c_re, ssm_c_im=ssm_c_im, ssm_d=ssm_d, ssm_w_glu=ssm_w_glu,
             kv_norm=kv_norm, w_kv=w_kv, k_norm=k_norm, attn_norm=attn_norm, w_q=w_q,
             q_norm=q_norm, w_o=w_o, mlp_norm=mlp_norm, w_up=w_up, w_down=w_down)

    pos_p = jnp.arange(SEQ, dtype=jnp.int32)

    def prompt_attend(q, k, v):
        return lax.map(lambda a: moba_attend_seq(a[0], pos_p, a[1], a[2]), (q, k, v))

    y_prompt, k_prompt, v_prompt, ssm_re_prompt, ssm_im_prompt = run_trunk(
        x_prompt, pos_p, None, None, prompt_attend, p)

    pos_s = PAST_LEN + jnp.arange(DEC_SEQ, dtype=jnp.int32)

    def sample_attend(q, k, v):
        def one(a):
            qb, kn, vn, pt = a
            kp = cache_k[pt].reshape(-1, N_HEADS, HEAD_DIM).astype(kn.dtype)
            vp = cache_v[pt].reshape(-1, N_HEADS, HEAD_DIM).astype(vn.dtype)
            return moba_attend_seq(qb, pos_s, jnp.concatenate([kp, kn], axis=0),
                                   jnp.concatenate([vp, vn], axis=0))
        return lax.map(one, (q, k, v, page_table))

    y_sample, k_sample, v_sample, ssm_re_sample, ssm_im_sample = run_trunk(
        x_sample, pos_s, state_ssm_re, state_ssm_im, sample_attend, p)

    return (y_prompt, y_sample, k_prompt, v_prompt, k_sample, v_sample,
            ssm_re_prompt, ssm_im_prompt, ssm_re_sample, ssm_im_sample)
```

```python
import functools
import math

import jax
import jax.numpy as jnp
from jax import lax
from jax.experimental import pallas as pl
from jax.experimental.pallas import tpu as pltpu

F32 = jnp.float32
BF16 = jnp.bfloat16

EPS = 1e-6
NEG_INF = -1e30
ROPE_THETA = 500000.0
MOBA_BLOCK = 256
MOBA_TOPK = 3

SUBLANES = 8
LANES = 128
VMEM_LIMIT = 56 * 1024 * 1024

SSM_CHUNK_GROUPS = 8
SSM_SCAN_LANES = 512
SSM_TIME_TILE = 32
MLP_ROWS = 512
KVQ_ROWS = 256
SAMPLE_PAGES_PER_STEP = 4


def _const_spec(shape):
    nd = len(shape)
    return pl.BlockSpec(shape, lambda *_: (0,) * nd, pipeline_mode=pl.Buffered(1))


def _params(sem):
    return pltpu.CompilerParams(dimension_semantics=sem, vmem_limit_bytes=VMEM_LIMIT)


def _rms(x):
    return x * lax.rsqrt(jnp.mean(x * x, axis=-1, keepdims=True) + EPS)


def _dot(a, b):
    return jnp.dot(a, b, preferred_element_type=F32)


def _dot_nt(a, b):
    return lax.dot_general(a, b, (((1,), (1,)), ((), ())), preferred_element_type=F32)


def _split_bf16(x):
    hi = x.astype(BF16)
    lo = (x - hi.astype(F32)).astype(BF16)
    return hi, lo


def _ssm_kernel(x_ref, h0_ref, g_ref, a_ref, bmat_ref, cmat_ref, d_ref, wglu_ref,
                out_ref, hfin_ref, s_ref, hst_ref, *, tt, gp, nkc, kc, nc):
    ti = pl.program_id(1)
    d_model = x_ref.shape[-1]
    rows = tt * SUBLANES

    @pl.when(ti == 0)
    def _():
        hst_ref[...] = h0_ref[0]

    x = x_ref[...].reshape(rows, d_model)
    u = _rms(x) * g_ref[...]
    ub = u.astype(BF16)

    for c in range(nkc):
        res = _dot(ub[:, c * kc:(c + 1) * kc], bmat_ref[c])
        s_ref[:, :, c * nc:(c + 1) * nc] = res[:, :nc].reshape(tt, SUBLANES, nc)
        s_ref[:, :, gp + c * nc:gp + (c + 1) * nc] = res[:, nc:].reshape(tt, SUBLANES, nc)

    w = SSM_SCAN_LANES
    for c in range(gp // w):
        re = slice(c * w, (c + 1) * w)
        im = slice(gp + c * w, gp + (c + 1) * w)
        ar = a_ref[:, re]
        ai = a_ref[:, im]

        def step(t, carry, re=re, im=im, ar=ar, ai=ai):
            hr, hi = carry
            nr = ar * hr - ai * hi + s_ref[t, :, re]
            ni = ar * hi + ai * hr + s_ref[t, :, im]
            s_ref[t, :, re] = nr
            s_ref[t, :, im] = ni
            return nr, ni

        hr, hi = lax.fori_loop(0, tt, step, (hst_ref[:, re], hst_ref[:, im]),
                               unroll=min(tt, 8))
        hst_ref[:, re] = hr
        hst_ref[:, im] = hi

    ys = []
    for c in range(nkc):
        sr = s_ref[:, :, c * nc:(c + 1) * nc].reshape(rows, nc).astype(BF16)
        si = s_ref[:, :, gp + c * nc:gp + (c + 1) * nc].reshape(rows, nc).astype(BF16)
        ys.append(_dot(sr, cmat_ref[c, :nc, :]) + _dot(si, cmat_ref[c, nc:, :]))
    y = jnp.concatenate(ys, axis=1) + d_ref[...] * u
    y = jax.nn.gelu(y, approximate=True)
    z = _dot(y.astype(BF16), wglu_ref[...])
    out = x + z[:, :d_model] * jax.nn.sigmoid(z[:, d_model:])
    out_ref[...] = out.reshape(tt, SUBLANES, d_model)

    @pl.when(ti == pl.num_programs(1) - 1)
    def _():
        hfin_ref[0] = hst_ref[...]


def _ssm_layer(x_tm, h0, g, a_b, bmat, cmat, d, wglu, tt):
    t_len, b_tot, d_model = x_tm.shape
    nbg = b_tot // SUBLANES
    gp2 = h0.shape[-1]
    gp = gp2 // 2
    nkc, kc, nc2 = bmat.shape
    kern = functools.partial(_ssm_kernel, tt=tt, gp=gp, nkc=nkc, kc=kc, nc=nc2 // 2)
    return pl.pallas_call(
        kern,
        grid=(nbg, t_len // tt),
        in_specs=[
            pl.BlockSpec((tt, SUBLANES, d_model), lambda b, t: (t, b, 0)),
            pl.BlockSpec((1, SUBLANES, gp2), lambda b, t: (b, 0, 0)),
            _const_spec(g.shape), _const_spec(a_b.shape), _const_spec(bmat.shape),
            _const_spec(cmat.shape), _const_spec(d.shape), _const_spec(wglu.shape),
        ],
        out_specs=[
            pl.BlockSpec((tt, SUBLANES, d_model), lambda b, t: (t, b, 0)),
            pl.BlockSpec((1, SUBLANES, gp2), lambda b, t: (b, 0, 0)),
        ],
        out_shape=[
            jax.ShapeDtypeStruct(x_tm.shape, F32),
            jax.ShapeDtypeStruct(h0.shape, F32),
        ],
        scratch_shapes=[
            pltpu.VMEM((tt, SUBLANES, gp2), F32),
            pltpu.VMEM((SUBLANES, gp2), F32),
        ],
        compiler_params=_params(("arbitrary", "arbitrary")),
    )(x_tm, h0, g, a_b, bmat, cmat, d, wglu)


def _ssm_weights(lam_re, lam_im, log_dt, b_re, b_im, c_re, c_im):
    g_n, p_n, c_n = b_re.shape
    dt = jnp.exp(log_dt)[:, None]
    mag = jnp.exp(lam_re * dt)
    ab_re, ab_im = mag * jnp.cos(lam_im * dt), mag * jnp.sin(lam_im * dt)
    nr, ni = ab_re - 1.0, ab_im
    den = lam_re * lam_re + lam_im * lam_im
    f_re = (nr * lam_re + ni * lam_im) / den
    f_im = (ni * lam_re - nr * lam_im) / den
    bb_re = f_re[..., None] * b_re - f_im[..., None] * b_im
    bb_im = f_re[..., None] * b_im + f_im[..., None] * b_re
    gk = SSM_CHUNK_GROUPS
    nkc = g_n // gk
    eye = jnp.eye(gk, dtype=F32)

    def bdiag_in(w):
        w = w.reshape(nkc, gk, p_n, c_n)
        return jnp.einsum('kgpc,gh->kgchp', w, eye).reshape(nkc, gk * c_n, gk * p_n)

    def bdiag_out(w):
        w = w.reshape(nkc, gk, c_n, p_n)
        return jnp.einsum('kgcp,gh->kgphc', w, eye).reshape(nkc, gk * p_n, gk * c_n)

    bmat = jnp.concatenate([bdiag_in(bb_re), bdiag_in(bb_im)], axis=2).astype(BF16)
    cmat = jnp.concatenate([bdiag_out(c_re), bdiag_out(-c_im)], axis=1).astype(BF16)
    a_row = jnp.concatenate([ab_re.reshape(-1), ab_im.reshape(-1)])
    a_b = jnp.broadcast_to(a_row[None, :], (SUBLANES, a_row.shape[0]))
    return a_b, bmat, cmat


def _mlp_kernel(*refs, pre):
    if pre:
        h_ref, o_ref, wo_ref, g_ref, wup_ref, wdn_ref, out_ref = refs
        h = h_ref[...] + _dot(o_ref[...].astype(BF16), wo_ref[...])
    else:
        h_ref, g_ref, wup_ref, wdn_ref, out_ref = refs
        h = h_ref[...]
    u = (_rms(h) * g_ref[...]).astype(BF16)
    a = jnp.square(jnp.maximum(_dot(u, wup_ref[...]), 0.0))
    out_ref[...] = h + _dot(a.astype(BF16), wdn_ref[...])


def _mlp(h, h_map, grid, rows, g, wup, wdn, out_rows, out_map, o=None, o_map=None, wo=None):
    d_model = g.shape[-1]
    blk = (rows, d_model)
    pre = o is not None
    in_specs = [pl.BlockSpec(blk, h_map)]
    args = [h]
    if pre:
        in_specs += [pl.BlockSpec(blk, o_map), _const_spec(wo.shape)]
        args += [o, wo]
    in_specs += [_const_spec(g.shape), _const_spec(wup.shape), _const_spec(wdn.shape)]
    args += [g, wup, wdn]
    return pl.pallas_call(
        functools.partial(_mlp_kernel, pre=pre),
        grid=grid,
        in_specs=in_specs,
        out_specs=pl.BlockSpec(blk, out_map),
        out_shape=jax.ShapeDtypeStruct((out_rows, d_model), F32),
        compiler_params=_params(("arbitrary",) * len(grid)),
    )(*args)


def _kvq_kernel(h_ref, rope_ref, gkv_ref, gq_ref, wkv_ref, wq_ref, kn_ref, qn_ref, e_ref,
                k_ref, v_ref, kb_ref, vb_ref, qb_ref, km_ref, *, head_dim):
    hd = k_ref.shape[-1]
    half = head_dim // 8
    hn = _rms(h_ref[...])
    kv = _dot((hn * gkv_ref[...]).astype(BF16), wkv_ref[...])
    q = _dot((hn * gq_ref[...]).astype(BF16), wq_ref[...])
    reps = hd // LANES
    rope = rope_ref[...]
    cos_t = jnp.tile(rope[:, :LANES], (1, reps))
    sin_lo = jnp.tile(rope[:, LANES:2 * LANES], (1, reps))
    sin_hi = jnp.tile(rope[:, 2 * LANES:], (1, reps))

    def headnorm_rope(x, gain):
        hi, lo = _split_bf16(x * x)
        ss = _dot(hi, e_ref[...]) + _dot(lo, e_ref[...])
        xn = x * lax.rsqrt(ss * (1.0 / head_dim) + EPS) * gain
        return (xn * cos_t + pltpu.roll(xn, hd - half, 1) * sin_lo
                + pltpu.roll(xn, half, 1) * sin_hi)

    k = headnorm_rope(kv[:, :hd], kn_ref[...])
    v = kv[:, hd:]
    q = headnorm_rope(q, qn_ref[...]) * (head_dim ** -0.5)
    k_ref[...] = k
    v_ref[...] = v
    kb_ref[...] = k.astype(BF16)
    vb_ref[...] = v.astype(BF16)
    qb_ref[...] = q.astype(BF16)
    km_ref[0] = jnp.mean(k, axis=0, keepdims=True)


def _kvq(h, h_map, grid, rows, rope, rope_map, out_2d, out_map, km_blocks, km_map,
         gkv, gq, wkv, wq, kn, qn, e_mat, head_dim):
    d_model = gkv.shape[-1]
    hd = wq.shape[-1]
    blk_o = (rows, hd)
    f32_out = jax.ShapeDtypeStruct(out_2d, F32)
    bf_out = jax.ShapeDtypeStruct(out_2d, BF16)
    return pl.pallas_call(
        functools.partial(_kvq_kernel, head_dim=head_dim),
        grid=grid,
        in_specs=[
            pl.BlockSpec((rows, d_model), h_map),
            pl.BlockSpec((rows, 3 * LANES), rope_map),
            _const_spec(gkv.shape), _const_spec(gq.shape), _const_spec(wkv.shape),
            _const_spec(wq.shape), _const_spec(kn.shape), _const_spec(qn.shape),
            _const_spec(e_mat.shape),
        ],
        out_specs=[pl.BlockSpec(blk_o, out_map)] * 5 + [pl.BlockSpec((1, 1, hd), km_map)],
        out_shape=[f32_out, f32_out, bf_out, bf_out, bf_out,
                   jax.ShapeDtypeStruct((km_blocks, 1, hd), F32)],
        compiler_params=_params(("arbitrary",) * len(grid)),
    )(h, rope, gkv, gq, wkv, wq, kn, qn, e_mat)


def _rope_table(pos, head_dim):
    rot = head_dim // 4
    half = rot // 2
    inv = ROPE_THETA ** (-jnp.arange(half, dtype=F32) / half)
    ang = pos.astype(F32)[:, None] * inv[None, :]
    cos, sin = jnp.cos(ang), jnp.sin(ang)
    j = jnp.arange(LANES) % head_dim
    cos_t = jnp.where(j[None, :] < rot, cos[:, j % half], 1.0)
    sin_lo = jnp.where(j[None, :] < half, -sin[:, j % half], 0.0)
    sin_hi = jnp.where((j[None, :] >= half) & (j[None, :] < rot), sin[:, j % half], 0.0)
    return jnp.concatenate([cos_t, sin_lo, sin_hi], axis=1)


def _moba_select(gate, lane_blk, n_past, per):
    rank = jnp.zeros(gate.shape, F32)
    for r in range(1, per):
        other = pltpu.roll(gate, r, 1)
        other_blk = (lane_blk + (per - r)) % per
        beats = ((other > gate) | ((other == gate) & (other_blk < lane_blk))) & (other_blk < n_past)
        rank = rank + jnp.where(beats, 1.0, 0.0)
    return (rank < MOBA_TOPK) & (lane_blk < n_past)


def _attn_prompt_kernel(q_ref, kb_ref, vb_ref, km_ref, ex_ref, o_ref, bias_ref, *, nb, head_dim):
    i = pl.program_id(1)
    blk = MOBA_BLOCK
    hd = q_ref.shape[-1]
    lane = lax.broadcasted_iota(jnp.int32, (1, LANES), 1)
    per = pl.next_power_of_2(nb)
    lane_blk = lane % per
    low = lane < head_dim
    row_i = lax.broadcasted_iota(jnp.int32, (blk, blk), 0)
    col_i = lax.broadcasted_iota(jnp.int32, (blk, blk), 1)
    causal = row_i >= col_i
    km = km_ref[0]
    outs = []
    for p in range(hd // LANES):
        cols = slice(p * LANES, (p + 1) * LANES)
        qp = q_ref[0, :, cols]
        kmp = km[:, cols]
        if per > nb:
            kmp = jnp.concatenate([kmp, jnp.zeros((per - nb, LANES), F32)], axis=0)
        km_hi, km_lo = _split_bf16(jnp.tile(kmp, (LANES // per, 1)))
        zero = jnp.zeros_like(qp)
        qm = (jnp.where(low, qp, zero), jnp.where(low, zero, qp))
        for hh in range(2):
            gate = _dot_nt(qm[hh], km_hi) + _dot_nt(qm[hh], km_lo)
            sel = _moba_select(gate, lane_blk, i, per) & (lane < nb)
            sel_all = _dot(jnp.where(sel, 1.0, 0.0).astype(BF16), ex_ref[...])
            for n in range(nb):
                bias_ref[hh, n] = jnp.where(sel_all[:, n * LANES:(n + 1) * LANES] > 0.5, 0.0, NEG_INF)

        def masked_v(v):
            zv = jnp.zeros_like(v)
            return (jnp.where(low, v, zv), jnp.where(low, zv, v))

        own = pl.ds(pl.multiple_of(i * blk, blk), blk)
        k_own = kb_ref[0, own, cols]
        v_own = masked_v(vb_ref[0, own, cols])
        ms, ls, pvs = [], [], []
        for hh in range(2):
            s = jnp.where(causal, _dot_nt(qm[hh], k_own), NEG_INF)
            m = jnp.max(s, axis=1, keepdims=True)
            pe = jnp.exp(s - m)
            ms.append(m)
            ls.append(jnp.sum(pe, axis=1, keepdims=True))
            pvs.append(_dot(pe.astype(BF16), v_own[hh]))

        def body(j, carry, cols=cols, qm=qm):
            m0, l0, m1, l1, acc = carry
            rows_j = pl.ds(pl.multiple_of(j * blk, blk), blk)
            k_j = kb_ref[0, rows_j, cols]
            v_j = masked_v(vb_ref[0, rows_j, cols])
            new = []
            for hh, (m, l) in enumerate(((m0, l0), (m1, l1))):
                s = _dot_nt(qm[hh], k_j) + jnp.tile(bias_ref[hh, j], (1, blk // LANES))
                mn = jnp.maximum(m, jnp.max(s, axis=1, keepdims=True))
                alpha = jnp.exp(m - mn)
                pe = jnp.exp(s - mn)
                new.append((mn, alpha * l + jnp.sum(pe, axis=1, keepdims=True), alpha,
                            _dot(pe.astype(BF16), v_j[hh])))
            acc = acc * jnp.where(low, new[0][2], new[1][2]) + new[0][3] + new[1][3]
            return new[0][0], new[0][1], new[1][0], new[1][1], acc

        m0, l0, m1, l1, acc = lax.fori_loop(0, i, body, (ms[0], ls[0], ms[1], ls[1], pvs[0] + pvs[1]))
        outs.append(acc / jnp.where(low, l0, l1))
    o_ref[0] = jnp.concatenate(outs, axis=1).astype(o_ref.dtype)


def _attn_prompt(qb, kb, vb, kmean, expand, head_dim):
    b_n, t_len, hd = qb.shape
    nb = t_len // MOBA_BLOCK
    return pl.pallas_call(
        functools.partial(_attn_prompt_kernel, nb=nb, head_dim=head_dim),
        grid=(b_n, nb),
        in_specs=[
            pl.BlockSpec((1, MOBA_BLOCK, hd), lambda b, i: (b, i, 0)),
            pl.BlockSpec((1, t_len, hd), lambda b, i: (b, 0, 0)),
            pl.BlockSpec((1, t_len, hd), lambda b, i: (b, 0, 0)),
            pl.BlockSpec((1, nb, hd), lambda b, i: (b, 0, 0)),
            _const_spec(expand.shape),
        ],
        out_specs=pl.BlockSpec((1, MOBA_BLOCK, hd), lambda b, i: (b, i, 0)),
        out_shape=jax.ShapeDtypeStruct((b_n, t_len, hd), BF16),
        scratch_shapes=[pltpu.VMEM((2, nb, MOBA_BLOCK, LANES), F32)],
        compiler_params=_params(("arbitrary", "arbitrary")),
    )(qb, kb, vb, kmean, expand)


def _attn_sample_kernel(pt_ref, *refs, npg, pps, n_new, head_dim):
    del pt_ref
    ck = refs[:pps]
    cv = refs[pps:2 * pps]
    q_ref, kn_ref, vn_ref, o_ref, s_ref, wt_ref, acc_ref = refs[2 * pps:]
    s_idx = pl.program_id(1)
    ksteps = npg // pps
    page = ck[0].shape[1]
    hd = q_ref.shape[-1]
    ncol = wt_ref.shape[0]
    ppb = MOBA_BLOCK // page
    n_past = npg // ppb

    @pl.when(s_idx == 0)
    def _():
        qt = jnp.tile(q_ref[0], (ncol // n_new, 1))
        r_head = lax.broadcasted_iota(jnp.int32, (ncol, hd), 0) // n_new
        c_head = lax.broadcasted_iota(jnp.int32, (ncol, hd), 1) // head_dim
        wt_ref[...] = jnp.where(r_head == c_head, qt, 0.0).astype(BF16)

    @pl.when(s_idx < ksteps)
    def _():
        for r in range(pps):
            s_ref[s_idx * pps + r] = _dot_nt(ck[r][0].astype(BF16), wt_ref[...])

    @pl.when(s_idx == ksteps)
    def _():
        sub = lax.broadcasted_iota(jnp.int32, (n_past, ncol), 0)
        gate = jnp.zeros((n_past, ncol), F32)
        for n in range(n_past):
            tot = jnp.sum(s_ref[n * ppb], axis=0, keepdims=True)
            for r in range(1, ppb):
                tot = tot + jnp.sum(s_ref[n * ppb + r], axis=0, keepdims=True)
            gate = jnp.where(sub == n, tot * (1.0 / MOBA_BLOCK), gate)
        rank = jnp.zeros((n_past, ncol), F32)
        for m in range(n_past):
            gm = gate[m:m + 1, :]
            beats = (gm > gate) | ((gm == gate) & (m < sub))
            rank = rank + jnp.where(beats, 1.0, 0.0)
        bias = jnp.where(rank < MOBA_TOPK, 0.0, NEG_INF)

        kn = jnp.concatenate([kn_ref[0], jnp.zeros((page - n_new, hd), F32)], axis=0)
        s_own = _dot_nt(kn.astype(BF16), wt_ref[...])
        key_t = lax.broadcasted_iota(jnp.int32, (page, ncol), 0)
        qry_t = lax.broadcasted_iota(jnp.int32, (page, ncol), 1) % n_new
        s_own = jnp.where(key_t <= qry_t, s_own, NEG_INF)

        mx = jnp.max(s_own, axis=0, keepdims=True)
        for pg in range(npg):
            sp = s_ref[pg] + bias[pg // ppb:pg // ppb + 1, :]
            s_ref[pg] = sp
            mx = jnp.maximum(mx, jnp.max(sp, axis=0, keepdims=True))
        p_own = jnp.exp(s_own - mx)
        den = jnp.sum(p_own, axis=0, keepdims=True)
        for pg in range(npg):
            pe = jnp.exp(s_ref[pg] - mx)
            s_ref[pg] = pe
            den = den + jnp.sum(pe, axis=0, keepdims=True)
        inv = 1.0 / den
        for pg in range(npg):
            s_ref[pg] = s_ref[pg] * inv
        vn = jnp.concatenate([vn_ref[0], jnp.zeros((page - n_new, hd), F32)], axis=0)
        acc_ref[...] = _dot((p_own * inv).T.astype(BF16), vn.astype(BF16))

    @pl.when(s_idx >= ksteps)
    def _():
        acc = acc_ref[...]
        for r in range(pps):
            pt_page = s_ref[(s_idx - ksteps) * pps + r].T.astype(BF16)
            acc = acc + _dot(pt_page, cv[r][0].astype(BF16))
        acc_ref[...] = acc

    @pl.when(s_idx == pl.num_programs(1) - 1)
    def _():
        r_head = lax.broadcasted_iota(jnp.int32, (ncol, hd), 0) // n_new
        c_head = lax.broadcasted_iota(jnp.int32, (ncol, hd), 1) // head_dim
        diag = jnp.where(r_head == c_head, acc_ref[...], 0.0)
        out = diag[0:n_new, :]
        for h in range(1, ncol // n_new):
            out = out + diag[h * n_new:(h + 1) * n_new, :]
        o_ref[0] = out


def _attn_sample(page_table, cache_k, cache_v, q, k_new, v_new, head_dim):
    bs, n_new, hd = q.shape
    npg = page_table.shape[1]
    page = cache_k.shape[1]
    pps = SAMPLE_PAGES_PER_STEP
    ksteps = npg // pps
    n_heads = hd // head_dim
    ncol = n_heads * n_new
    assert (npg * page) % MOBA_BLOCK == 0 and MOBA_BLOCK % page == 0 and n_new <= page
    assert npg % pps == 0 and ncol % SUBLANES == 0

    def k_map(r):
        return lambda b, s, pt: (pt[b, pps * jnp.minimum(s, ksteps - 1) + r], 0, 0)

    def v_map(r):
        return lambda b, s, pt: (pt[b, pps * jnp.maximum(s - ksteps, 0) + r], 0, 0)

    seq_spec = pl.BlockSpec((1, n_new, hd), lambda b, s, pt: (b, 0, 0))
    grid_spec = pltpu.PrefetchScalarGridSpec(
        num_scalar_prefetch=1,
        grid=(bs, 2 * ksteps),
        in_specs=([pl.BlockSpec((1, page, hd), k_map(r)) for r in range(pps)]
                  + [pl.BlockSpec((1, page, hd), v_map(r)) for r in range(pps)]
                  + [seq_spec, seq_spec, seq_spec]),
        out_specs=pl.BlockSpec((1, n_new, hd), lambda b, s, pt: (b, 0, 0)),
        scratch_shapes=[
            pltpu.VMEM((npg, page, ncol), F32),
            pltpu.VMEM((ncol, hd), BF16),
            pltpu.VMEM((ncol, hd), F32),
        ],
    )
    return pl.pallas_call(
        functools.partial(_attn_sample_kernel, npg=npg, pps=pps, n_new=n_new, head_dim=head_dim),
        grid_spec=grid_spec,
        out_shape=jax.ShapeDtypeStruct((bs, n_new, hd), F32),
        compiler_params=_params(("arbitrary", "arbitrary")),
    )(page_table, *([cache_k] * pps), *([cache_v] * pps), q, k_new, v_new)


def kernel(x_prompt, x_sample, state_ssm_re, state_ssm_im, cache_k, cache_v, page_table,
           ssm_norm, ssm_lambda_re, ssm_lambda_im, ssm_log_dt, ssm_b_re, ssm_b_im,
           ssm_c_re, ssm_c_im, ssm_d, ssm_w_glu, kv_norm, w_kv, k_norm,
           attn_norm, w_q, q_norm, w_o, mlp_norm, w_up, w_down):
    b_p, t_p, d_model = x_prompt.shape
    b_s, t_s, _ = x_sample.shape
    n_pool, page, n_heads, head_dim = cache_k.shape
    hd = n_heads * head_dim
    g_n, p_n = ssm_lambda_re.shape[1:]
    gp = g_n * p_n
    past_len = page_table.shape[1] * page
    assert ssm_norm.shape[0] == 1 and attn_norm.shape[0] == 1, "one S5 layer then one MoBA layer"
    assert b_p == SUBLANES and b_s % SUBLANES == 0 and t_p % MOBA_BLOCK == 0
    assert g_n % SSM_CHUNK_GROUPS == 0 and gp % SSM_SCAN_LANES == 0 and hd % LANES == 0
    assert t_p % SSM_TIME_TILE == 0 and t_p % MLP_ROWS == 0 and (t_s * b_s) % MLP_ROWS == 0

    row = lambda w: w.reshape(1, -1)
    a_b, bmat, cmat = _ssm_weights(ssm_lambda_re[0], ssm_lambda_im[0], ssm_log_dt[0],
                                   ssm_b_re[0], ssm_b_im[0], ssm_c_re[0], ssm_c_im[0])
    wglu = ssm_w_glu[0].astype(BF16)
    wup = w_up.astype(BF16)
    wdn = w_down.astype(BF16)
    wkv = w_kv.astype(BF16)
    wq = w_q[0].astype(BF16)
    wo = w_o[0].astype(BF16)
    head_of = jnp.arange(hd) // head_dim
    e_mat = (head_of[:, None] == head_of[None, :]).astype(BF16)
    kn_t = row(jnp.tile(k_norm, n_heads))
    qn_t = row(jnp.tile(q_norm[0], n_heads))

    xp_tm = jnp.transpose(x_prompt, (1, 0, 2))
    xs_tm = jnp.transpose(x_sample, (1, 0, 2))
    h0_p = jnp.zeros((b_p // SUBLANES, SUBLANES, 2 * gp), F32)
    h0_s = jnp.concatenate([state_ssm_re[0].reshape(b_s, gp), state_ssm_im[0].reshape(b_s, gp)],
                           axis=1).reshape(b_s // SUBLANES, SUBLANES, 2 * gp)
    ssm_args = (row(ssm_norm[0]), a_b, bmat, cmat, row(ssm_d[0]), wglu)
    hp, fin_p = _ssm_layer(xp_tm, h0_p, *ssm_args, tt=SSM_TIME_TILE)
    hs, fin_s = _ssm_layer(xs_tm, h0_s, *ssm_args, tt=t_s)

    def split_state(fin, b_n):
        fin = fin.reshape(b_n, 2 * gp)
        return fin[:, :gp].reshape(1, b_n, g_n, p_n), fin[:, gp:].reshape(1, b_n, g_n, p_n)

    ssm_re_p, ssm_im_p = split_state(fin_p, b_p)
    ssm_re_s, ssm_im_s = split_state(fin_s, b_s)

    n_p, n_s = t_p * b_p, t_s * b_s
    mlp0 = (row(mlp_norm[0]), wup[0], wdn[0])
    row_map = lambda i: (i, 0)
    hp = _mlp(hp.reshape(n_p, d_model), row_map, (n_p // MLP_ROWS,), MLP_ROWS, *mlp0, n_p, row_map)
    hs = _mlp(hs.reshape(n_s, d_model), row_map, (n_s // MLP_ROWS,), MLP_ROWS, *mlp0, n_s, row_map)

    kvq_w = (row(kv_norm), row(attn_norm[0]), wkv, wq, kn_t, qn_t, e_mat, head_dim)
    rope_p = _rope_table(jnp.arange(t_p, dtype=jnp.int32), head_dim)
    nt_p = t_p // KVQ_ROWS
    hp2 = hp.reshape(t_p, b_p * d_model)
    k_p, v_p, kb_p, vb_p, qb_p, km_p = _kvq(
        hp2, lambda b, t: (t, b), (b_p, nt_p), KVQ_ROWS, rope_p, lambda b, t: (t, 0),
        (b_p * t_p, hd), lambda b, t: (b * nt_p + t, 0), b_p * nt_p, lambda b, t: (b * nt_p + t, 0, 0),
        *kvq_w)
    pos_s = past_len + jnp.arange(t_s, dtype=jnp.int32)
    rope_s = jnp.repeat(_rope_table(pos_s, head_dim), b_s, axis=0)
    k_s, v_s, _, _, qb_s, _ = _kvq(
        hs, lambda t: (t, 0), (t_s,), b_s, rope_s, lambda t: (t, 0),
        (b_s, t_s * hd), lambda t: (0, t), t_s, lambda t: (t, 0, 0), *kvq_w)

    assert KVQ_ROWS == MOBA_BLOCK
    nb_p = t_p // MOBA_BLOCK
    expand = (jnp.arange(LANES)[:, None] == (jnp.arange(nb_p * LANES)[None, :] // LANES)).astype(BF16)
    o_p = _attn_prompt(qb_p.reshape(b_p, t_p, hd), kb_p.reshape(b_p, t_p, hd),
                       vb_p.reshape(b_p, t_p, hd), km_p.reshape(b_p, nb_p, hd), expand, head_dim)
    o_s = _attn_sample(page_table, cache_k.reshape(n_pool, page, hd), cache_v.reshape(n_pool, page, hd),
                       qb_s.reshape(b_s, t_s, hd).astype(F32), k_s.reshape(b_s, t_s, hd),
                       v_s.reshape(b_s, t_s, hd), head_dim)
    o_s_tm = jnp.transpose(o_s, (1, 0, 2)).reshape(n_s, hd)

    mlp1 = (row(mlp_norm[1]), wup[1], wdn[1])
    nt_m = t_p // MLP_ROWS
    y_p = _mlp(hp2, lambda b, t: (t, b), (b_p, nt_m), MLP_ROWS, *mlp1, b_p * t_p,
               lambda b, t: (b * nt_m + t, 0),
               o=o_p.reshape(b_p * t_p, hd), o_map=lambda b, t: (b * nt_m + t, 0), wo=wo)
    y_s = _mlp(hs, row_map, (n_s // MLP_ROWS,), MLP_ROWS, *mlp1, n_s, row_map,
               o=o_s_tm, o_map=row_map, wo=wo)
    y_s = jnp.transpose(y_s.reshape(t_s, b_s, d_model), (1, 0, 2))

    return (y_p.reshape(b_p, t_p, d_model), y_s,
            k_p.reshape(b_p, t_p, n_heads, head_dim), v_p.reshape(b_p, t_p, n_heads, head_dim),
            k_s.reshape(b_s, t_s, n_heads, head_dim), v_s.reshape(b_s, t_s, n_heads, head_dim),
            ssm_re_p, ssm_im_p, ssm_re_s, ssm_im_s)
```

```python
import functools

import jax
import jax.numpy as jnp
from jax import lax
from jax.experimental import pallas as pl
from jax.experimental.pallas import tpu as pltpu

F32 = jnp.float32
BF16 = jnp.bfloat16

EPS = 1e-6
NEG_INF = -1e30
ROPE_THETA = 500000.0
MOBA_BLOCK = 256
MOBA_TOPK = 3

SUBLANES = 8
LANES = 128
VMEM_LIMIT = 56 * 1024 * 1024

SSM_CHUNK_GROUPS = 8
SSM_SCAN_LANES = 512
SSM_TIME_TILE = 32
MLP_ROWS = 512
SAMPLE_KVQ_ROWS = 128
SAMPLE_PAGES_PER_STEP = 4


def _const_spec(shape):
    nd = len(shape)
    return pl.BlockSpec(shape, lambda *_: (0,) * nd, pipeline_mode=pl.Buffered(1))


def _params(sem):
    return pltpu.CompilerParams(dimension_semantics=sem, vmem_limit_bytes=VMEM_LIMIT)


def _rms(x):
    return x * lax.rsqrt(jnp.mean(x * x, axis=-1, keepdims=True) + EPS)


def _dot(a, b):
    return jnp.dot(a, b, preferred_element_type=F32)


def _dot_nt(a, b):
    return lax.dot_general(a, b, (((1,), (1,)), ((), ())), preferred_element_type=F32)


def _split_bf16(x):
    hi = x.astype(BF16)
    lo = (x - hi.astype(F32)).astype(BF16)
    return hi, lo


def _ssm_kernel(x_ref, h0_ref, g_ref, a_ref, bmat_ref, cmat_ref, d_ref, wglu_ref,
                out_ref, hfin_ref, s_ref, hst_ref, xs_ref, os_ref, *, tt, gp, nkc, kc, nc):
    ti = pl.program_id(1)
    d_model = x_ref.shape[-1]
    rows = tt * SUBLANES

    @pl.when(ti == 0)
    def _():
        hst_ref[...] = h0_ref[0]

    n_cb = d_model // LANES
    for b in range(SUBLANES):
        xb = x_ref[b]
        for c in range(n_cb):
            xs_ref[c, pl.ds(b, tt, stride=SUBLANES), :] = xb[:, c * LANES:(c + 1) * LANES]
    x = jnp.concatenate([xs_ref[c] for c in range(n_cb)], axis=1)
    u = _rms(x) * g_ref[...]
    ub = u.astype(BF16)

    for c in range(nkc):
        res = _dot(ub[:, c * kc:(c + 1) * kc], bmat_ref[c])
        s_ref[:, :, c * nc:(c + 1) * nc] = res[:, :nc].reshape(tt, SUBLANES, nc)
        s_ref[:, :, gp + c * nc:gp + (c + 1) * nc] = res[:, nc:].reshape(tt, SUBLANES, nc)

    w = SSM_SCAN_LANES
    for c in range(gp // w):
        re = slice(c * w, (c + 1) * w)
        im = slice(gp + c * w, gp + (c + 1) * w)
        ar = a_ref[:, re]
        ai = a_ref[:, im]

        def step(t, carry, re=re, im=im, ar=ar, ai=ai):
            hr, hi = carry
            nr = ar * hr - ai * hi + s_ref[t, :, re]
            ni = ar * hi + ai * hr + s_ref[t, :, im]
            s_ref[t, :, re] = nr
            s_ref[t, :, im] = ni
            return nr, ni

        hr, hi = lax.fori_loop(0, tt, step, (hst_ref[:, re], hst_ref[:, im]),
                               unroll=min(tt, 8))
        hst_ref[:, re] = hr
        hst_ref[:, im] = hi

    ys = []
    for c in range(nkc):
        sr = s_ref[:, :, c * nc:(c + 1) * nc].reshape(rows, nc).astype(BF16)
        si = s_ref[:, :, gp + c * nc:gp + (c + 1) * nc].reshape(rows, nc).astype(BF16)
        ys.append(_dot(sr, cmat_ref[c, :nc, :]) + _dot(si, cmat_ref[c, nc:, :]))
    y = jnp.concatenate(ys, axis=1) + d_ref[...] * u
    y = jax.nn.gelu(y, approximate=True)
    z = _dot(y.astype(BF16), wglu_ref[...])
    out = x + z[:, :d_model] * jax.nn.sigmoid(z[:, d_model:])
    for c in range(n_cb):
        os_ref[c] = out[:, c * LANES:(c + 1) * LANES]
    for b in range(SUBLANES):
        for c in range(n_cb):
            out_ref[b, :, c * LANES:(c + 1) * LANES] = os_ref[c, pl.ds(b, tt, stride=SUBLANES), :]

    @pl.when(ti == pl.num_programs(1) - 1)
    def _():
        hfin_ref[0] = hst_ref[...]


def _ssm_layer(x, h0, g, a_b, bmat, cmat, d, wglu, tt):
    b_tot, t_len, d_model = x.shape
    nbg = b_tot // SUBLANES
    gp2 = h0.shape[-1]
    gp = gp2 // 2
    nkc, kc, nc2 = bmat.shape
    kern = functools.partial(_ssm_kernel, tt=tt, gp=gp, nkc=nkc, kc=kc, nc=nc2 // 2)
    return pl.pallas_call(
        kern,
        grid=(nbg, t_len // tt),
        in_specs=[
            pl.BlockSpec((SUBLANES, tt, d_model), lambda b, t: (b, t, 0)),
            pl.BlockSpec((1, SUBLANES, gp2), lambda b, t: (b, 0, 0)),
            _const_spec(g.shape), _const_spec(a_b.shape), _const_spec(bmat.shape),
            _const_spec(cmat.shape), _const_spec(d.shape), _const_spec(wglu.shape),
        ],
        out_specs=[
            pl.BlockSpec((SUBLANES, tt, d_model), lambda b, t: (b, t, 0)),
            pl.BlockSpec((1, SUBLANES, gp2), lambda b, t: (b, 0, 0)),
        ],
        out_shape=[
            jax.ShapeDtypeStruct(x.shape, F32),
            jax.ShapeDtypeStruct(h0.shape, F32),
        ],
        scratch_shapes=[
            pltpu.VMEM((tt, SUBLANES, gp2), F32),
            pltpu.VMEM((SUBLANES, gp2), F32),
            pltpu.VMEM((d_model // LANES, tt * SUBLANES, LANES), F32),
            pltpu.VMEM((d_model // LANES, tt * SUBLANES, LANES), F32),
        ],
        compiler_params=_params(("arbitrary", "arbitrary")),
    )(x, h0, g, a_b, bmat, cmat, d, wglu)


def _ssm_weights(lam_re, lam_im, log_dt, b_re, b_im, c_re, c_im):
    g_n, p_n, c_n = b_re.shape
    dt = jnp.exp(log_dt)[:, None]
    mag = jnp.exp(lam_re * dt)
    ab_re, ab_im = mag * jnp.cos(lam_im * dt), mag * jnp.sin(lam_im * dt)
    nr, ni = ab_re - 1.0, ab_im
    den = lam_re * lam_re + lam_im * lam_im
    f_re = (nr * lam_re + ni * lam_im) / den
    f_im = (ni * lam_re - nr * lam_im) / den
    bb_re = f_re[..., None] * b_re - f_im[..., None] * b_im
    bb_im = f_re[..., None] * b_im + f_im[..., None] * b_re
    gk = SSM_CHUNK_GROUPS
    nkc = g_n // gk
    eye = jnp.eye(gk, dtype=F32)

    def bdiag_in(w):
        w = w.reshape(nkc, gk, p_n, c_n)
        return jnp.einsum('kgpc,gh->kgchp', w, eye).reshape(nkc, gk * c_n, gk * p_n)

    def bdiag_out(w):
        w = w.reshape(nkc, gk, c_n, p_n)
        return jnp.einsum('kgcp,gh->kgphc', w, eye).reshape(nkc, gk * p_n, gk * c_n)

    bmat = jnp.concatenate([bdiag_in(bb_re), bdiag_in(bb_im)], axis=2).astype(BF16)
    cmat = jnp.concatenate([bdiag_out(c_re), bdiag_out(-c_im)], axis=1).astype(BF16)
    a_row = jnp.concatenate([ab_re.reshape(-1), ab_im.reshape(-1)])
    a_b = jnp.broadcast_to(a_row[None, :], (SUBLANES, a_row.shape[0]))
    return a_b, bmat, cmat


def _mlp_kernel(*refs, pre):
    if pre:
        h_ref, o_ref, wo_ref, g_ref, wup_ref, wdn_ref, out_ref = refs
        h = h_ref[...] + _dot(o_ref[...].astype(BF16), wo_ref[...])
    else:
        h_ref, g_ref, wup_ref, wdn_ref, out_ref = refs
        h = h_ref[...]
    u = (_rms(h) * g_ref[...]).astype(BF16)
    a = jnp.square(jnp.maximum(_dot(u, wup_ref[...]), 0.0))
    out_ref[...] = h + _dot(a.astype(BF16), wdn_ref[...])


def _mlp(h, g, wup, wdn, o=None, wo=None):
    n_rows, d_model = h.shape
    blk = pl.BlockSpec((MLP_ROWS, d_model), lambda i: (i, 0))
    pre = o is not None
    in_specs = [blk]
    args = [h]
    if pre:
        in_specs += [pl.BlockSpec((MLP_ROWS, o.shape[-1]), lambda i: (i, 0)), _const_spec(wo.shape)]
        args += [o, wo]
    in_specs += [_const_spec(g.shape), _const_spec(wup.shape), _const_spec(wdn.shape)]
    args += [g, wup, wdn]
    return pl.pallas_call(
        functools.partial(_mlp_kernel, pre=pre),
        grid=(n_rows // MLP_ROWS,),
        in_specs=in_specs,
        out_specs=blk,
        out_shape=jax.ShapeDtypeStruct((n_rows, d_model), F32),
        compiler_params=_params(("arbitrary",)),
    )(*args)


def _kvq_kernel(h_ref, rope_ref, gkv_ref, gq_ref, wkv_ref, wq_ref, kn_ref, qn_ref, e_ref,
                k_ref, v_ref, *extra, head_dim, prompt):
    hd = wq_ref.shape[-1]
    half = head_dim // 8
    hn = _rms(h_ref[...])
    kv = _dot((hn * gkv_ref[...]).astype(BF16), wkv_ref[...])
    q = _dot((hn * gq_ref[...]).astype(BF16), wq_ref[...])
    reps = hd // LANES
    rope = rope_ref[...]
    cos_t = jnp.tile(rope[:, :LANES], (1, reps))
    sin_lo = jnp.tile(rope[:, LANES:2 * LANES], (1, reps))
    sin_hi = jnp.tile(rope[:, 2 * LANES:], (1, reps))

    def headnorm_rope(x, gain):
        hi, lo = _split_bf16(x * x)
        ss = _dot(hi, e_ref[...]) + _dot(lo, e_ref[...])
        xn = x * lax.rsqrt(ss * (1.0 / head_dim) + EPS) * gain
        return (xn * cos_t + pltpu.roll(xn, hd - half, 1) * sin_lo
                + pltpu.roll(xn, half, 1) * sin_hi)

    k = headnorm_rope(kv[:, :hd], kn_ref[...])
    v = kv[:, hd:]
    q = headnorm_rope(q, qn_ref[...]) * (head_dim ** -0.5)
    if prompt:
        kb_ref, qt_ref, vt_ref, km_ref = extra
        vt = v.T
        k_ref[0] = k.T
        v_ref[0] = vt
        kb_ref[...] = k.astype(BF16)
        qt_ref[0] = q.T.astype(BF16)
        vt_ref[0] = vt.astype(BF16)
        km_ref[0] = jnp.mean(k, axis=0, keepdims=True)
    else:
        (q_ref,) = extra
        k_ref[...] = k
        v_ref[...] = v
        q_ref[...] = q


def _kvq(h, rope, rope_blocks, rows, gkv, gq, wkv, wq, kn, qn, e_mat, head_dim, prompt):
    n_rows, d_model = h.shape
    hd = wq.shape[-1]
    n_blk = n_rows // rows
    row_map = lambda i: (i, 0)
    if prompt:
        kv_out = jax.ShapeDtypeStruct((n_blk // rope_blocks, hd, rope_blocks * rows), F32)
        out_specs = [pl.BlockSpec((1, hd, rows), lambda i: (i // rope_blocks, 0, i % rope_blocks))] * 2
    else:
        kv_out = jax.ShapeDtypeStruct((n_rows, hd), F32)
        out_specs = [pl.BlockSpec((rows, hd), row_map)] * 2
    out_shape = [kv_out, kv_out]
    if prompt:
        out_specs += [pl.BlockSpec((rows, hd), row_map),
                      pl.BlockSpec((1, hd, rows), lambda i: (i, 0, 0)),
                      pl.BlockSpec((1, hd, rows), lambda i: (i, 0, 0)),
                      pl.BlockSpec((1, 1, hd), lambda i: (i, 0, 0))]
        out_shape += [jax.ShapeDtypeStruct((n_rows, hd), BF16),
                      jax.ShapeDtypeStruct((n_blk, hd, rows), BF16),
                      jax.ShapeDtypeStruct((n_blk, hd, rows), BF16),
                      jax.ShapeDtypeStruct((n_blk, 1, hd), F32)]
    else:
        out_specs += [pl.BlockSpec((rows, hd), row_map)]
        out_shape += [jax.ShapeDtypeStruct((n_rows, hd), F32)]
    return pl.pallas_call(
        functools.partial(_kvq_kernel, head_dim=head_dim, prompt=prompt),
        grid=(n_blk,),
        in_specs=[
            pl.BlockSpec((rows, d_model), row_map),
            pl.BlockSpec((rows, 3 * LANES), lambda i: (i % rope_blocks, 0)),
            _const_spec(gkv.shape), _const_spec(gq.shape), _const_spec(wkv.shape),
            _const_spec(wq.shape), _const_spec(kn.shape), _const_spec(qn.shape),
            _const_spec(e_mat.shape),
        ],
        out_specs=out_specs,
        out_shape=out_shape,
        compiler_params=_params(("arbitrary",)),
    )(h, rope, gkv, gq, wkv, wq, kn, qn, e_mat)


def _rope_table(pos, head_dim):
    rot = head_dim // 4
    half = rot // 2
    inv = ROPE_THETA ** (-jnp.arange(half, dtype=F32) / half)
    ang = pos.astype(F32)[:, None] * inv[None, :]
    cos, sin = jnp.cos(ang), jnp.sin(ang)
    j = jnp.arange(LANES) % head_dim
    cos_t = jnp.where(j[None, :] < rot, cos[:, j % half], 1.0)
    sin_lo = jnp.where(j[None, :] < half, -sin[:, j % half], 0.0)
    sin_hi = jnp.where((j[None, :] >= half) & (j[None, :] < rot), sin[:, j % half], 0.0)
    return jnp.concatenate([cos_t, sin_lo, sin_hi], axis=1)


def _attn_prompt_kernel(qt_ref, kb_ref, vt_ref, km_ref, o_ref, bias_ref, *, nb, head_dim):
    i = pl.program_id(1)
    blk = MOBA_BLOCK
    hd = kb_ref.shape[-1]
    per = bias_ref.shape[1]
    blk_row = lax.broadcasted_iota(jnp.int32, (per, blk), 0)
    key_i = lax.broadcasted_iota(jnp.int32, (blk, blk), 0)
    qry_i = lax.broadcasted_iota(jnp.int32, (blk, blk), 1)
    causal = key_i <= qry_i
    low = lax.broadcasted_iota(jnp.int32, (LANES, 1), 0) < head_dim
    km = km_ref[0]
    own = pl.ds(pl.multiple_of(i * blk, blk), blk)
    outs = []
    for p in range(hd // LANES):
        cols = slice(p * LANES, (p + 1) * LANES)
        qt = qt_ref[0, 0, cols, :]
        zq = jnp.zeros_like(qt)
        qh = (jnp.where(low, qt, zq), jnp.where(low, zq, qt))
        kmp = km[:, cols]
        if per > nb:
            kmp = jnp.concatenate([kmp, jnp.zeros((per - nb, LANES), F32)], axis=0)
        km_hi, km_lo = _split_bf16(kmp)
        for hh in range(2):
            gate = _dot(km_hi, qh[hh]) + _dot(km_lo, qh[hh])
            rank = jnp.zeros((per, blk), F32)
            for m in range(nb - 1):
                gm = gate[m:m + 1, :]
                ge = jnp.where(gm >= gate, 1.0, 0.0)
                gt = jnp.where(gm > gate, 1.0, 0.0)
                rank = rank + jnp.where(blk_row > m, ge, gt) * jnp.where(m < i, 1.0, 0.0)
            sel = (rank < MOBA_TOPK) & (blk_row < i)
            bias_ref[hh] = jnp.where(sel, 0.0, NEG_INF)

        k_own = kb_ref[0, own, cols]
        v_own = vt_ref[0, i, cols, :]
        init = []
        for hh in range(2):
            s = jnp.where(causal, _dot(k_own, qh[hh]), NEG_INF)
            m = jnp.max(s, axis=0, keepdims=True)
            pe = jnp.exp(s - m)
            pv = _dot(v_own, pe.astype(BF16))
            init += [m, jnp.sum(pe, axis=0, keepdims=True),
                     pv[hh * head_dim:(hh + 1) * head_dim, :]]

        def body(j, carry, cols=cols, qh=qh):
            k_j = kb_ref[0, pl.ds(pl.multiple_of(j * blk, blk), blk), cols]
            v_j = vt_ref[0, j, cols, :]
            new = []
            for hh in range(2):
                m, l, acc = carry[3 * hh:3 * hh + 3]
                s = _dot(k_j, qh[hh])
                bj = bias_ref[hh, pl.ds(j, 1), :]
                mn = jnp.maximum(m, jnp.max(s, axis=0, keepdims=True) + bj)
                alpha = jnp.exp(m - mn)
                pe = jnp.exp(s + (bj - mn))
                pv = _dot(v_j, pe.astype(BF16))
                new += [mn, alpha * l + jnp.sum(pe, axis=0, keepdims=True),
                        acc * alpha + pv[hh * head_dim:(hh + 1) * head_dim, :]]
            return tuple(new)

        fin = lax.fori_loop(0, i, body, tuple(init))
        outs += [fin[2] / fin[1], fin[5] / fin[4]]
    o_ref[0] = jnp.concatenate(outs, axis=0).T.astype(o_ref.dtype)


def _attn_prompt(qt, kb, vt, kmean, head_dim):
    b_n, t_len, hd = kb.shape
    nb = t_len // MOBA_BLOCK
    per = max(SUBLANES, pl.next_power_of_2(nb))
    return pl.pallas_call(
        functools.partial(_attn_prompt_kernel, nb=nb, head_dim=head_dim),
        grid=(b_n, nb),
        in_specs=[
            pl.BlockSpec((1, 1, hd, MOBA_BLOCK), lambda b, i: (b, i, 0, 0)),
            pl.BlockSpec((1, t_len, hd), lambda b, i: (b, 0, 0)),
            pl.BlockSpec((1, nb, hd, MOBA_BLOCK), lambda b, i: (b, 0, 0, 0)),
            pl.BlockSpec((1, nb, hd), lambda b, i: (b, 0, 0)),
        ],
        out_specs=pl.BlockSpec((1, MOBA_BLOCK, hd), lambda b, i: (b, i, 0)),
        out_shape=jax.ShapeDtypeStruct((b_n, t_len, hd), BF16),
        scratch_shapes=[pltpu.VMEM((2, per, MOBA_BLOCK), F32)],
        compiler_params=_params(("arbitrary", "arbitrary")),
    )(qt, kb, vt, kmean)


def _attn_sample_kernel(pt_ref, *refs, npg, pps, n_new, head_dim):
    del pt_ref
    ck = refs[:pps]
    cv = refs[pps:2 * pps]
    q_ref, kn_ref, vn_ref, o_ref, s_ref, wt_ref, acc_ref = refs[2 * pps:]
    s_idx = pl.program_id(1)
    ksteps = npg // pps
    page = ck[0].shape[2]
    hd = q_ref.shape[-1]
    ncol = wt_ref.shape[0]
    ppb = MOBA_BLOCK // page
    n_past = npg // ppb

    @pl.when(s_idx == 0)
    def _():
        qt = jnp.tile(q_ref[0], (ncol // n_new, 1))
        r_head = lax.broadcasted_iota(jnp.int32, (ncol, hd), 0) // n_new
        c_head = lax.broadcasted_iota(jnp.int32, (ncol, hd), 1) // head_dim
        wt_ref[...] = jnp.where(r_head == c_head, qt, 0.0).astype(BF16)

    @pl.when(s_idx < ksteps)
    def _():
        for r in range(pps):
            s_ref[s_idx * pps + r] = _dot(wt_ref[...], ck[r][0].astype(BF16))

    @pl.when(s_idx == ksteps)
    def _():
        gates = []
        for n in range(n_past):
            tot = jnp.sum(s_ref[n * ppb], axis=1, keepdims=True)
            for r in range(1, ppb):
                tot = tot + jnp.sum(s_ref[n * ppb + r], axis=1, keepdims=True)
            gates.append(tot * (1.0 / MOBA_BLOCK))
        bias = []
        for n in range(n_past):
            rank = jnp.zeros((ncol, 1), F32)
            for m in range(n_past):
                if m != n:
                    beats = (gates[m] >= gates[n]) if m < n else (gates[m] > gates[n])
                    rank = rank + jnp.where(beats, 1.0, 0.0)
            bias.append(jnp.where(rank < MOBA_TOPK, 0.0, NEG_INF))

        kn = jnp.concatenate([kn_ref[0], jnp.zeros((page - n_new, hd), F32)], axis=0)
        s_own = _dot_nt(wt_ref[...], kn.astype(BF16))
        qry_t = lax.broadcasted_iota(jnp.int32, (ncol, page), 0) % n_new
        key_t = lax.broadcasted_iota(jnp.int32, (ncol, page), 1)
        s_own = jnp.where(key_t <= qry_t, s_own, NEG_INF)

        mx = jnp.max(s_own, axis=1, keepdims=True)
        for pg in range(npg):
            mx = jnp.maximum(mx, jnp.max(s_ref[pg], axis=1, keepdims=True) + bias[pg // ppb])
        p_own = jnp.exp(s_own - mx)
        den = jnp.sum(p_own, axis=1, keepdims=True)
        for pg in range(npg):
            pe = jnp.exp(s_ref[pg] + (bias[pg // ppb] - mx))
            s_ref[pg] = pe
            den = den + jnp.sum(pe, axis=1, keepdims=True)
        inv = 1.0 / den
        for pg in range(npg):
            s_ref[pg] = s_ref[pg] * inv
        vn = jnp.concatenate([vn_ref[0], jnp.zeros((page - n_new, hd), F32)], axis=0)
        acc_ref[...] = _dot((p_own * inv).astype(BF16), vn.astype(BF16))

    @pl.when(s_idx >= ksteps)
    def _():
        acc = acc_ref[...]
        for r in range(pps):
            p_page = s_ref[(s_idx - ksteps) * pps + r].astype(BF16)
            acc = acc + _dot_nt(p_page, cv[r][0].astype(BF16))
        acc_ref[...] = acc

    @pl.when(s_idx == pl.num_programs(1) - 1)
    def _():
        r_head = lax.broadcasted_iota(jnp.int32, (ncol, hd), 0) // n_new
        c_head = lax.broadcasted_iota(jnp.int32, (ncol, hd), 1) // head_dim
        diag = jnp.where(r_head == c_head, acc_ref[...], 0.0)
        out = diag[0:n_new, :]
        for h in range(1, ncol // n_new):
            out = out + diag[h * n_new:(h + 1) * n_new, :]
        o_ref[0] = out


def _attn_sample(page_table, cache_kt, cache_vt, q, k_new, v_new, head_dim):
    bs, n_new, hd = q.shape
    npg = page_table.shape[1]
    page = cache_kt.shape[2]
    pps = SAMPLE_PAGES_PER_STEP
    ksteps = npg // pps
    ncol = (hd // head_dim) * n_new
    assert (npg * page) % MOBA_BLOCK == 0 and MOBA_BLOCK % page == 0 and n_new <= page
    assert npg % pps == 0 and ncol % SUBLANES == 0 and n_new % SUBLANES == 0

    def k_map(r):
        return lambda b, s, pt: (pt[b, pps * jnp.minimum(s, ksteps - 1) + r], 0, 0)

    def v_map(r):
        return lambda b, s, pt: (pt[b, pps * jnp.maximum(s - ksteps, 0) + r], 0, 0)

    seq_spec = pl.BlockSpec((1, n_new, hd), lambda b, s, pt: (b, 0, 0))
    grid_spec = pltpu.PrefetchScalarGridSpec(
        num_scalar_prefetch=1,
        grid=(bs, 2 * ksteps),
        in_specs=([pl.BlockSpec((1, hd, page), k_map(r)) for r in range(pps)]
                  + [pl.BlockSpec((1, hd, page), v_map(r)) for r in range(pps)]
                  + [seq_spec, seq_spec, seq_spec]),
        out_specs=pl.BlockSpec((1, n_new, hd), lambda b, s, pt: (b, 0, 0)),
        scratch_shapes=[
            pltpu.VMEM((npg, ncol, page), F32),
            pltpu.VMEM((ncol, hd), BF16),
            pltpu.VMEM((ncol, hd), F32),
        ],
    )
    return pl.pallas_call(
        functools.partial(_attn_sample_kernel, npg=npg, pps=pps, n_new=n_new, head_dim=head_dim),
        grid_spec=grid_spec,
        out_shape=jax.ShapeDtypeStruct((bs, n_new, hd), F32),
        compiler_params=_params(("arbitrary", "arbitrary")),
    )(page_table, *([cache_kt] * pps), *([cache_vt] * pps), q, k_new, v_new)


def kernel(x_prompt, x_sample, state_ssm_re, state_ssm_im, cache_k, cache_v, page_table,
           ssm_norm, ssm_lambda_re, ssm_lambda_im, ssm_log_dt, ssm_b_re, ssm_b_im,
           ssm_c_re, ssm_c_im, ssm_d, ssm_w_glu, kv_norm, w_kv, k_norm,
           attn_norm, w_q, q_norm, w_o, mlp_norm, w_up, w_down):
    b_p, t_p, d_model = x_prompt.shape
    b_s, t_s, _ = x_sample.shape
    n_pool, page, n_heads, head_dim = cache_k.shape
    hd = n_heads * head_dim
    g_n, p_n = ssm_lambda_re.shape[1:]
    gp = g_n * p_n
    past_len = page_table.shape[1] * page
    n_p, n_s = t_p * b_p, t_s * b_s
    assert ssm_norm.shape[0] == 1 and attn_norm.shape[0] == 1, "one S5 layer then one MoBA layer"
    assert b_p % SUBLANES == 0 and b_s % SUBLANES == 0 and t_p % MOBA_BLOCK == 0
    assert g_n % SSM_CHUNK_GROUPS == 0 and gp % SSM_SCAN_LANES == 0 and hd % LANES == 0
    assert t_p % SSM_TIME_TILE == 0 and n_p % MLP_ROWS == 0 and n_s % MLP_ROWS == 0
    assert SAMPLE_KVQ_ROWS % t_s == 0 and n_s % SAMPLE_KVQ_ROWS == 0

    row = lambda w: w.reshape(1, -1)
    a_b, bmat, cmat = _ssm_weights(ssm_lambda_re[0], ssm_lambda_im[0], ssm_log_dt[0],
                                   ssm_b_re[0], ssm_b_im[0], ssm_c_re[0], ssm_c_im[0])
    wglu = ssm_w_glu[0].astype(BF16)
    wup = w_up.astype(BF16)
    wdn = w_down.astype(BF16)
    wkv = w_kv.astype(BF16)
    wq = w_q[0].astype(BF16)
    wo = w_o[0].astype(BF16)
    head_of = jnp.arange(hd) // head_dim
    e_mat = (head_of[:, None] == head_of[None, :]).astype(BF16)
    kn_t = row(jnp.tile(k_norm, n_heads))
    qn_t = row(jnp.tile(q_norm[0], n_heads))

    h0_p = jnp.zeros((b_p // SUBLANES, SUBLANES, 2 * gp), F32)
    h0_s = jnp.concatenate([state_ssm_re[0].reshape(b_s, gp), state_ssm_im[0].reshape(b_s, gp)],
                           axis=1).reshape(b_s // SUBLANES, SUBLANES, 2 * gp)
    ssm_args = (row(ssm_norm[0]), a_b, bmat, cmat, row(ssm_d[0]), wglu)
    hp, fin_p = _ssm_layer(x_prompt, h0_p, *ssm_args, tt=SSM_TIME_TILE)
    hs, fin_s = _ssm_layer(x_sample, h0_s, *ssm_args, tt=t_s)

    def split_state(fin, b_n):
        fin = fin.reshape(b_n, 2 * gp)
        return fin[:, :gp].reshape(1, b_n, g_n, p_n), fin[:, gp:].reshape(1, b_n, g_n, p_n)

    ssm_re_p, ssm_im_p = split_state(fin_p, b_p)
    ssm_re_s, ssm_im_s = split_state(fin_s, b_s)

    mlp0 = (row(mlp_norm[0]), wup[0], wdn[0])
    hp = _mlp(hp.reshape(n_p, d_model), *mlp0)
    hs = _mlp(hs.reshape(n_s, d_model), *mlp0)

    kvq_w = (row(kv_norm), row(attn_norm[0]), wkv, wq, kn_t, qn_t, e_mat, head_dim)
    nb_p = t_p // MOBA_BLOCK
    rope_p = _rope_table(jnp.arange(t_p, dtype=jnp.int32), head_dim)
    k_p, v_p, kb_p, qt_p, vt_p, km_p = _kvq(hp, rope_p, nb_p, MOBA_BLOCK, *kvq_w, prompt=True)
    pos_s = past_len + jnp.arange(t_s, dtype=jnp.int32)
    rope_s = jnp.tile(_rope_table(pos_s, head_dim), (SAMPLE_KVQ_ROWS // t_s, 1))
    k_s, v_s, q_s = _kvq(hs, rope_s, 1, SAMPLE_KVQ_ROWS, *kvq_w, prompt=False)

    o_p = _attn_prompt(qt_p.reshape(b_p, nb_p, hd, MOBA_BLOCK), kb_p.reshape(b_p, t_p, hd),
                       vt_p.reshape(b_p, nb_p, hd, MOBA_BLOCK), km_p.reshape(b_p, nb_p, hd), head_dim)
    cache_kt = jnp.transpose(cache_k, (0, 2, 3, 1)).reshape(n_pool, hd, page)
    cache_vt = jnp.transpose(cache_v, (0, 2, 3, 1)).reshape(n_pool, hd, page)
    o_s = _attn_sample(page_table, cache_kt, cache_vt, q_s.reshape(b_s, t_s, hd),
                       k_s.reshape(b_s, t_s, hd), v_s.reshape(b_s, t_s, hd), head_dim)

    mlp1 = (row(mlp_norm[1]), wup[1], wdn[1])
    y_p = _mlp(hp, *mlp1, o=o_p.reshape(n_p, hd), wo=wo)
    y_s = _mlp(hs, *mlp1, o=o_s.reshape(n_s, hd), wo=wo)

    to_bthd = lambda a: jnp.transpose(a.reshape(b_p, n_heads, head_dim, t_p), (0, 3, 1, 2))
    return (y_p.reshape(b_p, t_p, d_model), y_s.reshape(b_s, t_s, d_model),
            to_bthd(k_p), to_bthd(v_p),
            k_s.reshape(b_s, t_s, n_heads, head_dim), v_s.reshape(b_s, t_s, n_heads, head_dim),
            ssm_re_p, ssm_im_p, ssm_re_s, ssm_im_s)
```

```python
import functools

import jax
import jax.numpy as jnp
from jax import lax
from jax.experimental import pallas as pl
from jax.experimental.pallas import tpu as pltpu

F32 = jnp.float32
BF16 = jnp.bfloat16

EPS = 1e-6
NEG_INF = -1e30
ROPE_THETA = 500000.0
MOBA_BLOCK = 256
MOBA_TOPK = 3

SUBLANES = 8
LANES = 128
VMEM_LIMIT = 56 * 1024 * 1024

SSM_CHUNK_GROUPS = 8
SSM_SCAN_LANES = 512
SSM_TIME_TILE = 32
MLP_ROWS = 512
SAMPLE_KVQ_ROWS = 128


def _const_spec(shape):
    nd = len(shape)
    return pl.BlockSpec(shape, lambda *_: (0,) * nd, pipeline_mode=pl.Buffered(1))


def _params(sem):
    return pltpu.CompilerParams(dimension_semantics=sem, vmem_limit_bytes=VMEM_LIMIT)


def _rms(x):
    return x * lax.rsqrt(jnp.mean(x * x, axis=-1, keepdims=True) + EPS)


def _dot(a, b):
    return jnp.dot(a, b, preferred_element_type=F32)


def _dot_nt(a, b):
    return lax.dot_general(a, b, (((1,), (1,)), ((), ())), preferred_element_type=F32)


def _split_bf16(x):
    hi = x.astype(BF16)
    lo = (x - hi.astype(F32)).astype(BF16)
    return hi, lo


def _ssm_kernel(x_ref, h0_ref, g_ref, a_ref, bmat_ref, cmat_ref, d_ref, wglu_ref,
                out_ref, hfin_ref, s_ref, hst_ref, xs_ref, os_ref, *, tt, gp, nkc, kc, nc):
    ti = pl.program_id(1)
    d_model = x_ref.shape[-1]
    rows = tt * SUBLANES

    @pl.when(ti == 0)
    def _():
        hst_ref[...] = h0_ref[0]

    n_cb = d_model // LANES
    for b in range(SUBLANES):
        xb = x_ref[b]
        for c in range(n_cb):
            xs_ref[c, pl.ds(b, tt, stride=SUBLANES), :] = xb[:, c * LANES:(c + 1) * LANES]
    x = jnp.concatenate([xs_ref[c] for c in range(n_cb)], axis=1)
    u = _rms(x) * g_ref[...]
    ub = u.astype(BF16)

    for c in range(nkc):
        res = _dot(ub[:, c * kc:(c + 1) * kc], bmat_ref[c])
        s_ref[:, :, c * nc:(c + 1) * nc] = res[:, :nc].reshape(tt, SUBLANES, nc)
        s_ref[:, :, gp + c * nc:gp + (c + 1) * nc] = res[:, nc:].reshape(tt, SUBLANES, nc)

    w = SSM_SCAN_LANES
    for c in range(gp // w):
        re = slice(c * w, (c + 1) * w)
        im = slice(gp + c * w, gp + (c + 1) * w)
        ar = a_ref[:, re]
        ai = a_ref[:, im]

        def step(t, carry, re=re, im=im, ar=ar, ai=ai):
            hr, hi = carry
            nr = ar * hr - ai * hi + s_ref[t, :, re]
            ni = ar * hi + ai * hr + s_ref[t, :, im]
            s_ref[t, :, re] = nr
            s_ref[t, :, im] = ni
            return nr, ni

        hr, hi = lax.fori_loop(0, tt, step, (hst_ref[:, re], hst_ref[:, im]),
                               unroll=min(tt, 8))
        hst_ref[:, re] = hr
        hst_ref[:, im] = hi

    ys = []
    for c in range(nkc):
        sr = s_ref[:, :, c * nc:(c + 1) * nc].reshape(rows, nc).astype(BF16)
        si = s_ref[:, :, gp + c * nc:gp + (c + 1) * nc].reshape(rows, nc).astype(BF16)
        ys.append(_dot(sr, cmat_ref[c, :nc, :]) + _dot(si, cmat_ref[c, nc:, :]))
    y = jnp.concatenate(ys, axis=1) + d_ref[...] * u
    y = jax.nn.gelu(y, approximate=True)
    z = _dot(y.astype(BF16), wglu_ref[...])
    out = x + z[:, :d_model] * jax.nn.sigmoid(z[:, d_model:])
    for c in range(n_cb):
        os_ref[c] = out[:, c * LANES:(c + 1) * LANES]
    for b in range(SUBLANES):
        for c in range(n_cb):
            out_ref[b, :, c * LANES:(c + 1) * LANES] = os_ref[c, pl.ds(b, tt, stride=SUBLANES), :]

    @pl.when(ti == pl.num_programs(1) - 1)
    def _():
        hfin_ref[0] = hst_ref[...]


def _ssm_layer(x, h0, g, a_b, bmat, cmat, d, wglu, tt):
    b_tot, t_len, d_model = x.shape
    nbg = b_tot // SUBLANES
    gp2 = h0.shape[-1]
    gp = gp2 // 2
    nkc, kc, nc2 = bmat.shape
    kern = functools.partial(_ssm_kernel, tt=tt, gp=gp, nkc=nkc, kc=kc, nc=nc2 // 2)
    return pl.pallas_call(
        kern,
        grid=(nbg, t_len // tt),
        in_specs=[
            pl.BlockSpec((SUBLANES, tt, d_model), lambda b, t: (b, t, 0)),
            pl.BlockSpec((1, SUBLANES, gp2), lambda b, t: (b, 0, 0)),
            _const_spec(g.shape), _const_spec(a_b.shape), _const_spec(bmat.shape),
            _const_spec(cmat.shape), _const_spec(d.shape), _const_spec(wglu.shape),
        ],
        out_specs=[
            pl.BlockSpec((SUBLANES, tt, d_model), lambda b, t: (b, t, 0)),
            pl.BlockSpec((1, SUBLANES, gp2), lambda b, t: (b, 0, 0)),
        ],
        out_shape=[
            jax.ShapeDtypeStruct(x.shape, F32),
            jax.ShapeDtypeStruct(h0.shape, F32),
        ],
        scratch_shapes=[
            pltpu.VMEM((tt, SUBLANES, gp2), F32),
            pltpu.VMEM((SUBLANES, gp2), F32),
            pltpu.VMEM((d_model // LANES, tt * SUBLANES, LANES), F32),
            pltpu.VMEM((d_model // LANES, tt * SUBLANES, LANES), F32),
        ],
        compiler_params=_params(("arbitrary", "arbitrary")),
    )(x, h0, g, a_b, bmat, cmat, d, wglu)


def _ssm_weights(lam_re, lam_im, log_dt, b_re, b_im, c_re, c_im):
    g_n, p_n, c_n = b_re.shape
    dt = jnp.exp(log_dt)[:, None]
    mag = jnp.exp(lam_re * dt)
    ab_re, ab_im = mag * jnp.cos(lam_im * dt), mag * jnp.sin(lam_im * dt)
    nr, ni = ab_re - 1.0, ab_im
    den = lam_re * lam_re + lam_im * lam_im
    f_re = (nr * lam_re + ni * lam_im) / den
    f_im = (ni * lam_re - nr * lam_im) / den
    bb_re = f_re[..., None] * b_re - f_im[..., None] * b_im
    bb_im = f_re[..., None] * b_im + f_im[..., None] * b_re
    gk = SSM_CHUNK_GROUPS
    nkc = g_n // gk
    eye = jnp.eye(gk, dtype=F32)

    def bdiag_in(w):
        w = w.reshape(nkc, gk, p_n, c_n)
        return jnp.einsum('kgpc,gh->kgchp', w, eye).reshape(nkc, gk * c_n, gk * p_n)

    def bdiag_out(w):
        w = w.reshape(nkc, gk, c_n, p_n)
        return jnp.einsum('kgcp,gh->kgphc', w, eye).reshape(nkc, gk * p_n, gk * c_n)

    bmat = jnp.concatenate([bdiag_in(bb_re), bdiag_in(bb_im)], axis=2).astype(BF16)
    cmat = jnp.concatenate([bdiag_out(c_re), bdiag_out(-c_im)], axis=1).astype(BF16)
    a_row = jnp.concatenate([ab_re.reshape(-1), ab_im.reshape(-1)])
    a_b = jnp.broadcast_to(a_row[None, :], (SUBLANES, a_row.shape[0]))
    return a_b, bmat, cmat


def _mlp_kernel(*refs, pre):
    if pre:
        h_ref, o_ref, wo_ref, g_ref, wup_ref, wdn_ref, out_ref = refs
        h = h_ref[...] + _dot(o_ref[...].astype(BF16), wo_ref[...])
    else:
        h_ref, g_ref, wup_ref, wdn_ref, out_ref = refs
        h = h_ref[...]
    u = (_rms(h) * g_ref[...]).astype(BF16)
    a = jnp.square(jnp.maximum(_dot(u, wup_ref[...]), 0.0))
    out_ref[...] = h + _dot(a.astype(BF16), wdn_ref[...])


def _mlp(h, g, wup, wdn, o=None, wo=None):
    n_rows, d_model = h.shape
    blk = pl.BlockSpec((MLP_ROWS, d_model), lambda i: (i, 0))
    pre = o is not None
    in_specs = [blk]
    args = [h]
    if pre:
        in_specs += [pl.BlockSpec((MLP_ROWS, o.shape[-1]), lambda i: (i, 0)), _const_spec(wo.shape)]
        args += [o, wo]
    in_specs += [_const_spec(g.shape), _const_spec(wup.shape), _const_spec(wdn.shape)]
    args += [g, wup, wdn]
    return pl.pallas_call(
        functools.partial(_mlp_kernel, pre=pre),
        grid=(n_rows // MLP_ROWS,),
        in_specs=in_specs,
        out_specs=blk,
        out_shape=jax.ShapeDtypeStruct((n_rows, d_model), F32),
        compiler_params=_params(("arbitrary",)),
    )(*args)


def _kvq_kernel(h_ref, rope_ref, gkv_ref, gq_ref, wkv_ref, wq_ref, kn_ref, qn_ref, e_ref,
                k_ref, v_ref, *extra, head_dim, prompt):
    hd = wq_ref.shape[-1]
    half = head_dim // 8
    hn = _rms(h_ref[...])
    kv = _dot((hn * gkv_ref[...]).astype(BF16), wkv_ref[...])
    q = _dot((hn * gq_ref[...]).astype(BF16), wq_ref[...])
    reps = hd // LANES
    rope = rope_ref[...]
    cos_t = jnp.tile(rope[:, :LANES], (1, reps))
    sin_lo = jnp.tile(rope[:, LANES:2 * LANES], (1, reps))
    sin_hi = jnp.tile(rope[:, 2 * LANES:], (1, reps))

    def headnorm_rope(x, gain):
        hi, lo = _split_bf16(x * x)
        ss = _dot(hi, e_ref[...]) + _dot(lo, e_ref[...])
        xn = x * lax.rsqrt(ss * (1.0 / head_dim) + EPS) * gain
        return (xn * cos_t + pltpu.roll(xn, hd - half, 1) * sin_lo
                + pltpu.roll(xn, half, 1) * sin_hi)

    k = headnorm_rope(kv[:, :hd], kn_ref[...])
    v = kv[:, hd:]
    q = headnorm_rope(q, qn_ref[...]) * (head_dim ** -0.5)
    if prompt:
        kb_ref, qt_ref, vt_ref, km_ref = extra
        vt = v.T
        k_ref[0] = k.T
        v_ref[0] = vt
        kb = k.astype(BF16)
        for p in range(hd // LANES):
            kb_ref[0, p] = kb[:, p * LANES:(p + 1) * LANES]
        qt_ref[0] = q.T.astype(BF16)
        vt_ref[0] = vt.astype(BF16)
        km_ref[0] = jnp.mean(k, axis=0, keepdims=True)
    else:
        (q_ref,) = extra
        k_ref[...] = k
        v_ref[...] = v
        q_ref[...] = q


def _kvq(h, rope, rope_blocks, rows, gkv, gq, wkv, wq, kn, qn, e_mat, head_dim, prompt):
    n_rows, d_model = h.shape
    hd = wq.shape[-1]
    n_blk = n_rows // rows
    row_map = lambda i: (i, 0)
    if prompt:
        n_seq, t_len = n_blk // rope_blocks, rope_blocks * rows
        t_spec = pl.BlockSpec((1, hd, rows), lambda i: (i // rope_blocks, 0, i % rope_blocks))
        kv_out = jax.ShapeDtypeStruct((n_seq, hd, t_len), F32)
        out_specs = [t_spec] * 2
    else:
        kv_out = jax.ShapeDtypeStruct((n_rows, hd), F32)
        out_specs = [pl.BlockSpec((rows, hd), row_map)] * 2
    out_shape = [kv_out, kv_out]
    if prompt:
        out_specs += [pl.BlockSpec((1, hd // LANES, rows, LANES),
                                   lambda i: (i // rope_blocks, 0, i % rope_blocks, 0)),
                      t_spec, t_spec,
                      pl.BlockSpec((1, 1, hd), lambda i: (i, 0, 0))]
        out_shape += [jax.ShapeDtypeStruct((n_seq, hd // LANES, t_len, LANES), BF16),
                      jax.ShapeDtypeStruct((n_seq, hd, t_len), BF16),
                      jax.ShapeDtypeStruct((n_seq, hd, t_len), BF16),
                      jax.ShapeDtypeStruct((n_blk, 1, hd), F32)]
    else:
        out_specs += [pl.BlockSpec((rows, hd), row_map)]
        out_shape += [jax.ShapeDtypeStruct((n_rows, hd), F32)]
    return pl.pallas_call(
        functools.partial(_kvq_kernel, head_dim=head_dim, prompt=prompt),
        grid=(n_blk,),
        in_specs=[
            pl.BlockSpec((rows, d_model), row_map),
            pl.BlockSpec((rows, 3 * LANES), lambda i: (i % rope_blocks, 0)),
            _const_spec(gkv.shape), _const_spec(gq.shape), _const_spec(wkv.shape),
            _const_spec(wq.shape), _const_spec(kn.shape), _const_spec(qn.shape),
            _const_spec(e_mat.shape),
        ],
        out_specs=out_specs,
        out_shape=out_shape,
        compiler_params=_params(("arbitrary",)),
    )(h, rope, gkv, gq, wkv, wq, kn, qn, e_mat)


def _rope_table(pos, head_dim):
    rot = head_dim // 4
    half = rot // 2
    inv = ROPE_THETA ** (-jnp.arange(half, dtype=F32) / half)
    ang = pos.astype(F32)[:, None] * inv[None, :]
    cos, sin = jnp.cos(ang), jnp.sin(ang)
    j = jnp.arange(LANES) % head_dim
    cos_t = jnp.where(j[None, :] < rot, cos[:, j % half], 1.0)
    sin_lo = jnp.where(j[None, :] < half, -sin[:, j % half], 0.0)
    sin_hi = jnp.where((j[None, :] >= half) & (j[None, :] < rot), sin[:, j % half], 0.0)
    return jnp.concatenate([cos_t, sin_lo, sin_hi], axis=1)


def _attn_pair(n_past, p, qt_ref, kb_ref, vt_ref, km_ref, ot_ref, head_dim):
    blk = MOBA_BLOCK
    per = km_ref.shape[2]
    nk = (n_past + 1) * blk
    low = lax.broadcasted_iota(jnp.int32, (LANES, 1), 0) < head_dim
    qt = qt_ref[0, p]
    zq = jnp.zeros_like(qt)
    qh = (jnp.where(low, qt, zq), jnp.where(low, zq, qt))
    k_all = kb_ref[0, p, 0:nk, :]
    v_all = vt_ref[0, p, :, 0:nk]
    causal = (lax.broadcasted_iota(jnp.int32, (blk, blk), 0)
              <= lax.broadcasted_iota(jnp.int32, (blk, blk), 1))
    gated = n_past > MOBA_TOPK
    if gated:
        km_hi, km_lo = _split_bf16(km_ref[0, p])
        blk_row = lax.broadcasted_iota(jnp.int32, (per, blk), 0)
    outs = []
    for hh in range(2):
        s = _dot(k_all, qh[hh])
        pieces = []
        if gated:
            gate = _dot(km_hi, qh[hh]) + _dot(km_lo, qh[hh])
            rank = jnp.zeros((per, blk), F32)
            for m in range(n_past):
                gm = gate[m:m + 1, :]
                rank = rank + jnp.where(blk_row > m, jnp.where(gm >= gate, 1.0, 0.0),
                                        jnp.where(gm > gate, 1.0, 0.0))
            bias = jnp.where(rank < MOBA_TOPK, 0.0, NEG_INF)
            for j in range(n_past):
                pieces.append(s[j * blk:(j + 1) * blk, :] + bias[j:j + 1, :])
        else:
            for j in range(n_past):
                pieces.append(s[j * blk:(j + 1) * blk, :])
        pieces.append(jnp.where(causal, s[n_past * blk:, :], NEG_INF))
        m = jnp.max(pieces[0], axis=0, keepdims=True)
        for piece in pieces[1:]:
            m = jnp.maximum(m, jnp.max(piece, axis=0, keepdims=True))
        pes = [jnp.exp(piece - m) for piece in pieces]
        den = jnp.sum(pes[0], axis=0, keepdims=True)
        for pe in pes[1:]:
            den = den + jnp.sum(pe, axis=0, keepdims=True)
        pe_all = jnp.concatenate([pe.astype(BF16) for pe in pes], axis=0) if n_past else pes[0].astype(BF16)
        pv = _dot(v_all, pe_all)
        outs.append(pv[hh * head_dim:(hh + 1) * head_dim, :] / den)
    ot_ref[p] = jnp.concatenate(outs, axis=0)


def _attn_prompt_kernel(qt_ref, kb_ref, vt_ref, km_ref, o_ref, ot_ref, *, nb, head_dim):
    i = pl.program_id(1)
    n_pairs = ot_ref.shape[0]
    for n in range(nb):
        @pl.when(i == n)
        def _(n=n):
            def pair(p, carry):
                _attn_pair(n, p, qt_ref, kb_ref, vt_ref, km_ref, ot_ref, head_dim)
                return carry
            lax.fori_loop(0, n_pairs, pair, 0)
    o_ref[0] = jnp.concatenate([ot_ref[p] for p in range(n_pairs)], axis=0).T.astype(o_ref.dtype)


def _attn_prompt(qt, kb, vt, kmean, head_dim):
    b_n, n_pairs, t_len, _ = kb.shape
    hd = n_pairs * LANES
    nb = t_len // MOBA_BLOCK
    per = kmean.shape[2]
    return pl.pallas_call(
        functools.partial(_attn_prompt_kernel, nb=nb, head_dim=head_dim),
        grid=(b_n, nb),
        in_specs=[
            pl.BlockSpec((1, n_pairs, LANES, MOBA_BLOCK), lambda b, i: (b, 0, 0, i)),
            pl.BlockSpec((1, n_pairs, t_len, LANES), lambda b, i: (b, 0, 0, 0)),
            pl.BlockSpec((1, n_pairs, LANES, t_len), lambda b, i: (b, 0, 0, 0)),
            pl.BlockSpec((1, n_pairs, per, LANES), lambda b, i: (b, 0, 0, 0)),
        ],
        out_specs=pl.BlockSpec((1, MOBA_BLOCK, hd), lambda b, i: (b, i, 0)),
        out_shape=jax.ShapeDtypeStruct((b_n, t_len, hd), BF16),
        scratch_shapes=[pltpu.VMEM((n_pairs, LANES, MOBA_BLOCK), F32)],
        compiler_params=_params(("arbitrary", "arbitrary")),
    )(qt, kb, vt, kmean)


def _attn_sample_kernel(pt_ref, *refs, npg, n_new, head_dim):
    del pt_ref
    ck = refs[:npg]
    cv = refs[npg:2 * npg]
    q_ref, kn_ref, vn_ref, o_ref, s_ref = refs[2 * npg:]
    page = ck[0].shape[2]
    hd = q_ref.shape[-1]
    ncol = s_ref.shape[1]
    ppb = MOBA_BLOCK // page
    n_past = npg // ppb

    r_head = lax.broadcasted_iota(jnp.int32, (ncol, hd), 0) // n_new
    c_head = lax.broadcasted_iota(jnp.int32, (ncol, hd), 1) // head_dim
    own_head = r_head == c_head
    wt = jnp.where(own_head, jnp.tile(q_ref[0], (ncol // n_new, 1)), 0.0).astype(BF16)

    for pg in range(npg):
        s_ref[pg] = _dot(wt, ck[pg][0].astype(BF16))

    gates = []
    for n in range(n_past):
        tot = jnp.sum(s_ref[n * ppb], axis=1, keepdims=True)
        for r in range(1, ppb):
            tot = tot + jnp.sum(s_ref[n * ppb + r], axis=1, keepdims=True)
        gates.append(tot * (1.0 / MOBA_BLOCK))
    bias = []
    for n in range(n_past):
        rank = jnp.zeros((ncol, 1), F32)
        for m in range(n_past):
            if m != n:
                beats = (gates[m] >= gates[n]) if m < n else (gates[m] > gates[n])
                rank = rank + jnp.where(beats, 1.0, 0.0)
        bias.append(jnp.where(rank < MOBA_TOPK, 0.0, NEG_INF))

    kn = jnp.concatenate([kn_ref[0], jnp.zeros((page - n_new, hd), F32)], axis=0)
    s_own = _dot_nt(wt, kn.astype(BF16))
    qry_t = lax.broadcasted_iota(jnp.int32, (ncol, page), 0) % n_new
    key_t = lax.broadcasted_iota(jnp.int32, (ncol, page), 1)
    s_own = jnp.where(key_t <= qry_t, s_own, NEG_INF)

    mx = jnp.max(s_own, axis=1, keepdims=True)
    for pg in range(npg):
        mx = jnp.maximum(mx, jnp.max(s_ref[pg], axis=1, keepdims=True) + bias[pg // ppb])
    p_own = jnp.exp(s_own - mx)
    den = jnp.sum(p_own, axis=1, keepdims=True)
    for pg in range(npg):
        pe = jnp.exp(s_ref[pg] + (bias[pg // ppb] - mx))
        s_ref[pg] = pe
        den = den + jnp.sum(pe, axis=1, keepdims=True)
    vn = jnp.concatenate([vn_ref[0], jnp.zeros((page - n_new, hd), F32)], axis=0)
    acc = _dot(p_own.astype(BF16), vn.astype(BF16))
    for pg in range(npg):
        acc = acc + _dot_nt(s_ref[pg].astype(BF16), cv[pg][0].astype(BF16))
    diag = jnp.where(own_head, acc / den, 0.0)
    out = diag[0:n_new, :]
    for h in range(1, ncol // n_new):
        out = out + diag[h * n_new:(h + 1) * n_new, :]
    o_ref[0] = out


def _attn_sample(page_table, cache_kt, cache_vt, q, k_new, v_new, head_dim):
    bs, n_new, hd = q.shape
    npg = page_table.shape[1]
    page = cache_kt.shape[2]
    ncol = (hd // head_dim) * n_new
    assert (npg * page) % MOBA_BLOCK == 0 and MOBA_BLOCK % page == 0 and n_new <= page
    assert ncol % SUBLANES == 0 and n_new % SUBLANES == 0

    def page_spec(r):
        return pl.BlockSpec((1, hd, page), lambda b, pt: (pt[b, r], 0, 0))

    seq_spec = pl.BlockSpec((1, n_new, hd), lambda b, pt: (b, 0, 0))
    grid_spec = pltpu.PrefetchScalarGridSpec(
        num_scalar_prefetch=1,
        grid=(bs,),
        in_specs=[page_spec(r) for r in range(npg)] * 2 + [seq_spec, seq_spec, seq_spec],
        out_specs=seq_spec,
        scratch_shapes=[pltpu.VMEM((npg, ncol, page), F32)],
    )
    return pl.pallas_call(
        functools.partial(_attn_sample_kernel, npg=npg, n_new=n_new, head_dim=head_dim),
        grid_spec=grid_spec,
        out_shape=jax.ShapeDtypeStruct((bs, n_new, hd), F32),
        compiler_params=_params(("arbitrary",)),
    )(page_table, *([cache_kt] * npg), *([cache_vt] * npg), q, k_new, v_new)


def kernel(x_prompt, x_sample, state_ssm_re, state_ssm_im, cache_k, cache_v, page_table,
           ssm_norm, ssm_lambda_re, ssm_lambda_im, ssm_log_dt, ssm_b_re, ssm_b_im,
           ssm_c_re, ssm_c_im, ssm_d, ssm_w_glu, kv_norm, w_kv, k_norm,
           attn_norm, w_q, q_norm, w_o, mlp_norm, w_up, w_down):
    b_p, t_p, d_model = x_prompt.shape
    b_s, t_s, _ = x_sample.shape
    n_pool, page, n_heads, head_dim = cache_k.shape
    hd = n_heads * head_dim
    g_n, p_n = ssm_lambda_re.shape[1:]
    gp = g_n * p_n
    past_len = page_table.shape[1] * page
    n_p, n_s = t_p * b_p, t_s * b_s
    assert ssm_norm.shape[0] == 1 and attn_norm.shape[0] == 1, "one S5 layer then one MoBA layer"
    assert b_p % SUBLANES == 0 and b_s % SUBLANES == 0 and t_p % MOBA_BLOCK == 0
    assert g_n % SSM_CHUNK_GROUPS == 0 and gp % SSM_SCAN_LANES == 0 and hd % LANES == 0
    assert t_p % SSM_TIME_TILE == 0 and n_p % MLP_ROWS == 0 and n_s % MLP_ROWS == 0
    assert SAMPLE_KVQ_ROWS % t_s == 0 and n_s % SAMPLE_KVQ_ROWS == 0

    row = lambda w: w.reshape(1, -1)
    a_b, bmat, cmat = _ssm_weights(ssm_lambda_re[0], ssm_lambda_im[0], ssm_log_dt[0],
                                   ssm_b_re[0], ssm_b_im[0], ssm_c_re[0], ssm_c_im[0])
    wglu = ssm_w_glu[0].astype(BF16)
    wup = w_up.astype(BF16)
    wdn = w_down.astype(BF16)
    wkv = w_kv.astype(BF16)
    wq = w_q[0].astype(BF16)
    wo = w_o[0].astype(BF16)
    head_of = jnp.arange(hd) // head_dim
    e_mat = (head_of[:, None] == head_of[None, :]).astype(BF16)
    kn_t = row(jnp.tile(k_norm, n_heads))
    qn_t = row(jnp.tile(q_norm[0], n_heads))

    h0_p = jnp.zeros((b_p // SUBLANES, SUBLANES, 2 * gp), F32)
    h0_s = jnp.concatenate([state_ssm_re[0].reshape(b_s, gp), state_ssm_im[0].reshape(b_s, gp)],
                           axis=1).reshape(b_s // SUBLANES, SUBLANES, 2 * gp)
    ssm_args = (row(ssm_norm[0]), a_b, bmat, cmat, row(ssm_d[0]), wglu)
    hp, fin_p = _ssm_layer(x_prompt, h0_p, *ssm_args, tt=SSM_TIME_TILE)
    hs, fin_s = _ssm_layer(x_sample, h0_s, *ssm_args, tt=t_s)

    def split_state(fin, b_n):
        fin = fin.reshape(b_n, 2 * gp)
        return fin[:, :gp].reshape(1, b_n, g_n, p_n), fin[:, gp:].reshape(1, b_n, g_n, p_n)

    ssm_re_p, ssm_im_p = split_state(fin_p, b_p)
    ssm_re_s, ssm_im_s = split_state(fin_s, b_s)

    mlp0 = (row(mlp_norm[0]), wup[0], wdn[0])
    hp = _mlp(hp.reshape(n_p, d_model), *mlp0)
    hs = _mlp(hs.reshape(n_s, d_model), *mlp0)

    kvq_w = (row(kv_norm), row(attn_norm[0]), wkv, wq, kn_t, qn_t, e_mat, head_dim)
    nb_p = t_p // MOBA_BLOCK
    rope_p = _rope_table(jnp.arange(t_p, dtype=jnp.int32), head_dim)
    k_p, v_p, kb_p, qt_p, vt_p, km_p = _kvq(hp, rope_p, nb_p, MOBA_BLOCK, *kvq_w, prompt=True)
    pos_s = past_len + jnp.arange(t_s, dtype=jnp.int32)
    rope_s = jnp.tile(_rope_table(pos_s, head_dim), (SAMPLE_KVQ_ROWS // t_s, 1))
    k_s, v_s, q_s = _kvq(hs, rope_s, 1, SAMPLE_KVQ_ROWS, *kvq_w, prompt=False)

    n_pairs = hd // LANES
    km_p = jnp.transpose(km_p.reshape(b_p, nb_p, n_pairs, LANES), (0, 2, 1, 3))
    km_p = jnp.pad(km_p, ((0, 0), (0, 0), (0, -nb_p % SUBLANES), (0, 0)))
    o_p = _attn_prompt(qt_p.reshape(b_p, n_pairs, LANES, t_p), kb_p,
                       vt_p.reshape(b_p, n_pairs, LANES, t_p), km_p, head_dim)
    cache_kt = jnp.transpose(cache_k, (0, 2, 3, 1)).reshape(n_pool, hd, page)
    cache_vt = jnp.transpose(cache_v, (0, 2, 3, 1)).reshape(n_pool, hd, page)
    o_s = _attn_sample(page_table, cache_kt, cache_vt, q_s.reshape(b_s, t_s, hd),
                       k_s.reshape(b_s, t_s, hd), v_s.reshape(b_s, t_s, hd), head_dim)

    mlp1 = (row(mlp_norm[1]), wup[1], wdn[1])
    y_p = _mlp(hp, *mlp1, o=o_p.reshape(n_p, hd), wo=wo)
    y_s = _mlp(hs, *mlp1, o=o_s.reshape(n_s, hd), wo=wo)

    to_bthd = lambda a: jnp.transpose(a.reshape(b_p, n_heads, head_dim, t_p), (0, 3, 1, 2))
    return (y_p.reshape(b_p, t_p, d_model), y_s.reshape(b_s, t_s, d_model),
            to_bthd(k_p), to_bthd(v_p),
            k_s.reshape(b_s, t_s, n_heads, head_dim), v_s.reshape(b_s, t_s, n_heads, head_dim),
            ssm_re_p, ssm_im_p, ssm_re_s, ssm_im_s)
```

```python
import functools

import jax
import jax.numpy as jnp
from jax import lax
from jax.experimental import pallas as pl
from jax.experimental.pallas import tpu as pltpu

F32 = jnp.float32
BF16 = jnp.bfloat16

EPS = 1e-6
NEG_INF = -1e30
ROPE_THETA = 500000.0
MOBA_BLOCK = 256
MOBA_TOPK = 3
LOG2E = 1.4426950408889634

SUBLANES = 8
LANES = 128
VMEM_LIMIT = 56 * 1024 * 1024

SSM_CHUNK_GROUPS = 8
SSM_SCAN_LANES = 512
SSM_TIME_TILE = 32
MLP_ROWS = 512
SAMPLE_KVQ_ROWS = 128
ATTN_PAIRS_PER_TRIP = 2


def _const_spec(shape):
    nd = len(shape)
    return pl.BlockSpec(shape, lambda *_: (0,) * nd, pipeline_mode=pl.Buffered(1))


def _params(sem):
    return pltpu.CompilerParams(dimension_semantics=sem, vmem_limit_bytes=VMEM_LIMIT)


def _rms(x):
    return x * lax.rsqrt(jnp.mean(x * x, axis=-1, keepdims=True) + EPS)


def _dot(a, b):
    return jnp.dot(a, b, preferred_element_type=F32)


def _dot_nt(a, b):
    return lax.dot_general(a, b, (((1,), (1,)), ((), ())), preferred_element_type=F32)


def _split_bf16(x):
    hi = x.astype(BF16)
    lo = (x - hi.astype(F32)).astype(BF16)
    return hi, lo


def _ssm_kernel(x_ref, h0_ref, g_ref, a_ref, bmat_ref, cmat_ref, d_ref, wglu_ref,
                out_ref, hfin_ref, s_ref, hst_ref, xs_ref, os_ref, *, tt, gp, nkc, kc, nc):
    ti = pl.program_id(1)
    d_model = x_ref.shape[-1]
    rows = tt * SUBLANES

    @pl.when(ti == 0)
    def _():
        hst_ref[...] = h0_ref[0]

    n_cb = d_model // LANES
    for b in range(SUBLANES):
        xb = x_ref[b]
        for c in range(n_cb):
            xs_ref[c, pl.ds(b, tt, stride=SUBLANES), :] = xb[:, c * LANES:(c + 1) * LANES]
    x = jnp.concatenate([xs_ref[c] for c in range(n_cb)], axis=1)
    u = _rms(x) * g_ref[...]
    ub = u.astype(BF16)

    for c in range(nkc):
        res = _dot(ub[:, c * kc:(c + 1) * kc], bmat_ref[c])
        s_ref[:, :, c * nc:(c + 1) * nc] = res[:, :nc].reshape(tt, SUBLANES, nc)
        s_ref[:, :, gp + c * nc:gp + (c + 1) * nc] = res[:, nc:].reshape(tt, SUBLANES, nc)

    w = SSM_SCAN_LANES
    for c in range(gp // w):
        re = slice(c * w, (c + 1) * w)
        im = slice(gp + c * w, gp + (c + 1) * w)
        ar = a_ref[:, re]
        ai = a_ref[:, im]

        def step(t, carry, re=re, im=im, ar=ar, ai=ai):
            hr, hi = carry
            nr = ar * hr - ai * hi + s_ref[t, :, re]
            ni = ar * hi + ai * hr + s_ref[t, :, im]
            s_ref[t, :, re] = nr
            s_ref[t, :, im] = ni
            return nr, ni

        hr, hi = lax.fori_loop(0, tt, step, (hst_ref[:, re], hst_ref[:, im]),
                               unroll=min(tt, 8))
        hst_ref[:, re] = hr
        hst_ref[:, im] = hi

    ys = []
    for c in range(nkc):
        sr = s_ref[:, :, c * nc:(c + 1) * nc].reshape(rows, nc).astype(BF16)
        si = s_ref[:, :, gp + c * nc:gp + (c + 1) * nc].reshape(rows, nc).astype(BF16)
        ys.append(_dot(sr, cmat_ref[c, :nc, :]) + _dot(si, cmat_ref[c, nc:, :]))
    y = jnp.concatenate(ys, axis=1) + d_ref[...] * u
    y = jax.nn.gelu(y, approximate=True)
    z = _dot(y.astype(BF16), wglu_ref[...])
    out = x + z[:, :d_model] * jax.nn.sigmoid(z[:, d_model:])
    for c in range(n_cb):
        os_ref[c] = out[:, c * LANES:(c + 1) * LANES]
    for b in range(SUBLANES):
        for c in range(n_cb):
            out_ref[b, :, c * LANES:(c + 1) * LANES] = os_ref[c, pl.ds(b, tt, stride=SUBLANES), :]

    @pl.when(ti == pl.num_programs(1) - 1)
    def _():
        hfin_ref[0] = hst_ref[...]


def _ssm_layer(x, h0, g, a_b, bmat, cmat, d, wglu, tt):
    b_tot, t_len, d_model = x.shape
    nbg = b_tot // SUBLANES
    gp2 = h0.shape[-1]
    gp = gp2 // 2
    nkc, kc, nc2 = bmat.shape
    kern = functools.partial(_ssm_kernel, tt=tt, gp=gp, nkc=nkc, kc=kc, nc=nc2 // 2)
    return pl.pallas_call(
        kern,
        grid=(nbg, t_len // tt),
        in_specs=[
            pl.BlockSpec((SUBLANES, tt, d_model), lambda b, t: (b, t, 0)),
            pl.BlockSpec((1, SUBLANES, gp2), lambda b, t: (b, 0, 0)),
            _const_spec(g.shape), _const_spec(a_b.shape), _const_spec(bmat.shape),
            _const_spec(cmat.shape), _const_spec(d.shape), _const_spec(wglu.shape),
        ],
        out_specs=[
            pl.BlockSpec((SUBLANES, tt, d_model), lambda b, t: (b, t, 0)),
            pl.BlockSpec((1, SUBLANES, gp2), lambda b, t: (b, 0, 0)),
        ],
        out_shape=[
            jax.ShapeDtypeStruct(x.shape, F32),
            jax.ShapeDtypeStruct(h0.shape, F32),
        ],
        scratch_shapes=[
            pltpu.VMEM((tt, SUBLANES, gp2), F32),
            pltpu.VMEM((SUBLANES, gp2), F32),
            pltpu.VMEM((d_model // LANES, tt * SUBLANES, LANES), F32),
            pltpu.VMEM((d_model // LANES, tt * SUBLANES, LANES), F32),
        ],
        compiler_params=_params(("arbitrary", "arbitrary")),
    )(x, h0, g, a_b, bmat, cmat, d, wglu)


def _ssm_weights(lam_re, lam_im, log_dt, b_re, b_im, c_re, c_im):
    g_n, p_n, c_n = b_re.shape
    dt = jnp.exp(log_dt)[:, None]
    mag = jnp.exp(lam_re * dt)
    ab_re, ab_im = mag * jnp.cos(lam_im * dt), mag * jnp.sin(lam_im * dt)
    nr, ni = ab_re - 1.0, ab_im
    den = lam_re * lam_re + lam_im * lam_im
    f_re = (nr * lam_re + ni * lam_im) / den
    f_im = (ni * lam_re - nr * lam_im) / den
    bb_re = f_re[..., None] * b_re - f_im[..., None] * b_im
    bb_im = f_re[..., None] * b_im + f_im[..., None] * b_re
    gk = SSM_CHUNK_GROUPS
    nkc = g_n // gk
    eye = jnp.eye(gk, dtype=F32)

    def bdiag_in(w):
        w = w.reshape(nkc, gk, p_n, c_n)
        return jnp.einsum('kgpc,gh->kgchp', w, eye).reshape(nkc, gk * c_n, gk * p_n)

    def bdiag_out(w):
        w = w.reshape(nkc, gk, c_n, p_n)
        return jnp.einsum('kgcp,gh->kgphc', w, eye).reshape(nkc, gk * p_n, gk * c_n)

    bmat = jnp.concatenate([bdiag_in(bb_re), bdiag_in(bb_im)], axis=2).astype(BF16)
    cmat = jnp.concatenate([bdiag_out(c_re), bdiag_out(-c_im)], axis=1).astype(BF16)
    a_row = jnp.concatenate([ab_re.reshape(-1), ab_im.reshape(-1)])
    a_b = jnp.broadcast_to(a_row[None, :], (SUBLANES, a_row.shape[0]))
    return a_b, bmat, cmat


def _mlp_kernel(*refs, pre):
    if pre:
        h_ref, o_ref, wo_ref, g_ref, wup_ref, wdn_ref, out_ref = refs
        h = h_ref[...] + _dot(o_ref[...].astype(BF16), wo_ref[...])
    else:
        h_ref, g_ref, wup_ref, wdn_ref, out_ref = refs
        h = h_ref[...]
    u = (_rms(h) * g_ref[...]).astype(BF16)
    a = jnp.square(jnp.maximum(_dot(u, wup_ref[...]), 0.0))
    out_ref[...] = h + _dot(a.astype(BF16), wdn_ref[...])


def _mlp(h, g, wup, wdn, o=None, wo=None):
    n_rows, d_model = h.shape
    blk = pl.BlockSpec((MLP_ROWS, d_model), lambda i: (i, 0))
    pre = o is not None
    in_specs = [blk]
    args = [h]
    if pre:
        in_specs += [pl.BlockSpec((MLP_ROWS, o.shape[-1]), lambda i: (i, 0)), _const_spec(wo.shape)]
        args += [o, wo]
    in_specs += [_const_spec(g.shape), _const_spec(wup.shape), _const_spec(wdn.shape)]
    args += [g, wup, wdn]
    return pl.pallas_call(
        functools.partial(_mlp_kernel, pre=pre),
        grid=(n_rows // MLP_ROWS,),
        in_specs=in_specs,
        out_specs=blk,
        out_shape=jax.ShapeDtypeStruct((n_rows, d_model), F32),
        compiler_params=_params(("arbitrary",)),
    )(*args)


def _kvq_kernel(h_ref, rope_ref, gkv_ref, gq_ref, wkv_ref, wq_ref, kn_ref, qn_ref, e_ref,
                k_ref, v_ref, q_ref, *, head_dim):
    hd = wq_ref.shape[-1]
    half = head_dim // 8
    hn = _rms(h_ref[...])
    kv = _dot((hn * gkv_ref[...]).astype(BF16), wkv_ref[...])
    q = _dot((hn * gq_ref[...]).astype(BF16), wq_ref[...])
    reps = hd // LANES
    rope = rope_ref[...]
    cos_t = jnp.tile(rope[:, :LANES], (1, reps))
    sin_lo = jnp.tile(rope[:, LANES:2 * LANES], (1, reps))
    sin_hi = jnp.tile(rope[:, 2 * LANES:], (1, reps))

    def headnorm_rope(x, gain):
        hi, lo = _split_bf16(x * x)
        ss = _dot(hi, e_ref[...]) + _dot(lo, e_ref[...])
        xn = x * lax.rsqrt(ss * (1.0 / head_dim) + EPS) * gain
        return (xn * cos_t + pltpu.roll(xn, hd - half, 1) * sin_lo
                + pltpu.roll(xn, half, 1) * sin_hi)

    k_ref[...] = headnorm_rope(kv[:, :hd], kn_ref[...])
    v_ref[...] = kv[:, hd:]
    q_ref[...] = headnorm_rope(q, qn_ref[...]) * (head_dim ** -0.5 * LOG2E)


def _kvq_sample(h, rope, rows, gkv, gq, wkv, wq, kn, qn, e_mat, head_dim):
    n_rows, d_model = h.shape
    hd = wq.shape[-1]
    row_map = lambda i: (i, 0)
    out = jax.ShapeDtypeStruct((n_rows, hd), F32)
    return pl.pallas_call(
        functools.partial(_kvq_kernel, head_dim=head_dim),
        grid=(n_rows // rows,),
        in_specs=[
            pl.BlockSpec((rows, d_model), row_map),
            _const_spec(rope.shape),
            _const_spec(gkv.shape), _const_spec(gq.shape), _const_spec(wkv.shape),
            _const_spec(wq.shape), _const_spec(kn.shape), _const_spec(qn.shape),
            _const_spec(e_mat.shape),
        ],
        out_specs=[pl.BlockSpec((rows, hd), row_map)] * 3,
        out_shape=[out, out, out],
        compiler_params=_params(("arbitrary",)),
    )(h, rope, gkv, gq, wkv, wq, kn, qn, e_mat)


def _kvq_prompt_kernel(h_ref, rope_ref, gkv_ref, gq_ref, wkv_ref, wq_ref, kn_ref, qn_ref,
                       k_ref, v_ref, kb_ref, qt_ref, vt_ref, km_ref, *, head_dim):
    hd = wq_ref.shape[0]
    rows = h_ref.shape[0]
    n_heads = hd // head_dim
    half = head_dim // 8
    hn = _rms(h_ref[...])
    kvt = _dot_nt(wkv_ref[...], (hn * gkv_ref[...]).astype(BF16))
    qt = _dot_nt(wq_ref[...], (hn * gq_ref[...]).astype(BF16))
    cos = rope_ref[0:half, :][None]
    sin = rope_ref[half:2 * half, :][None]

    def headnorm_rope_t(xt, gain):
        x3 = xt.reshape(n_heads, head_dim, rows)
        ss = jnp.sum(x3 * x3, axis=1, keepdims=True)
        xn = x3 * lax.rsqrt(ss * (1.0 / head_dim) + EPS) * gain[None]
        lo, hi = xn[:, 0:half, :], xn[:, half:2 * half, :]
        out = jnp.concatenate([lo * cos - hi * sin, hi * cos + lo * sin, xn[:, 2 * half:, :]], axis=1)
        return out.reshape(hd, rows)

    kt = headnorm_rope_t(kvt[:hd, :], kn_ref[...])
    vt = kvt[hd:, :]
    k_ref[0] = kt
    v_ref[0] = vt
    qt_ref[0] = headnorm_rope_t(qt, qn_ref[...]).astype(BF16)
    vt_ref[0] = vt.astype(BF16)
    k = kt.T
    km_ref[0] = jnp.mean(k, axis=0, keepdims=True)
    kb = k.astype(BF16)
    for p in range(hd // LANES):
        kb_ref[0, p] = kb[:, p * LANES:(p + 1) * LANES]


def _kvq_prompt(h, rope_t, n_seq, gkv, gq, wkv, wq, kn_b, qn_b, head_dim):
    n_rows, d_model = h.shape
    hd = wq.shape[0]
    rows = MOBA_BLOCK
    t_len = n_rows // n_seq
    nb = t_len // rows
    t_spec = pl.BlockSpec((1, hd, rows), lambda i: (i // nb, 0, i % nb))
    t_f32 = jax.ShapeDtypeStruct((n_seq, hd, t_len), F32)
    t_bf16 = jax.ShapeDtypeStruct((n_seq, hd, t_len), BF16)
    return pl.pallas_call(
        functools.partial(_kvq_prompt_kernel, head_dim=head_dim),
        grid=(n_seq * nb,),
        in_specs=[
            pl.BlockSpec((rows, d_model), lambda i: (i, 0)),
            pl.BlockSpec((rope_t.shape[0], rows), lambda i: (0, i % nb)),
            _const_spec(gkv.shape), _const_spec(gq.shape), _const_spec(wkv.shape),
            _const_spec(wq.shape), _const_spec(kn_b.shape), _const_spec(qn_b.shape),
        ],
        out_specs=[t_spec, t_spec,
                   pl.BlockSpec((1, hd // LANES, rows, LANES), lambda i: (i // nb, 0, i % nb, 0)),
                   t_spec, t_spec,
                   pl.BlockSpec((1, 1, hd), lambda i: (i, 0, 0))],
        out_shape=[t_f32, t_f32,
                   jax.ShapeDtypeStruct((n_seq, hd // LANES, t_len, LANES), BF16),
                   t_bf16, t_bf16,
                   jax.ShapeDtypeStruct((n_seq * nb, 1, hd), F32)],
        compiler_params=_params(("arbitrary",)),
    )(h, rope_t, gkv, gq, wkv, wq, kn_b, qn_b)


def _rope_cos_sin(pos, head_dim):
    half = head_dim // 8
    inv = ROPE_THETA ** (-jnp.arange(half, dtype=F32) / half)
    ang = pos.astype(F32)[:, None] * inv[None, :]
    return jnp.cos(ang), jnp.sin(ang)


def _rope_table(pos, head_dim):
    rot = head_dim // 4
    half = rot // 2
    cos, sin = _rope_cos_sin(pos, head_dim)
    j = jnp.arange(LANES) % head_dim
    cos_t = jnp.where(j[None, :] < rot, cos[:, j % half], 1.0)
    sin_lo = jnp.where(j[None, :] < half, -sin[:, j % half], 0.0)
    sin_hi = jnp.where((j[None, :] >= half) & (j[None, :] < rot), sin[:, j % half], 0.0)
    return jnp.concatenate([cos_t, sin_lo, sin_hi], axis=1)


def _attn_pairs(n_past, ps, qt_ref, kb_ref, vt_ref, km_ref, ot_ref, head_dim):
    blk = MOBA_BLOCK
    per = km_ref.shape[2]
    nk = (n_past + 1) * blk
    low = lax.broadcasted_iota(jnp.int32, (LANES, 1), 0) < head_dim
    causal = (lax.broadcasted_iota(jnp.int32, (blk, blk), 0)
              <= lax.broadcasted_iota(jnp.int32, (blk, blk), 1))
    gated = n_past > MOBA_TOPK
    blk_row = lax.broadcasted_iota(jnp.int32, (per, blk), 0)

    scores = []
    for p in ps:
        qt = qt_ref[0, p]
        zq = jnp.zeros_like(qt)
        k_all = kb_ref[0, p, 0:nk, :]
        for qh in (jnp.where(low, qt, zq), jnp.where(low, zq, qt)):
            s = _dot(k_all, qh)
            bias = None
            if gated:
                km_hi, km_lo = _split_bf16(km_ref[0, p])
                gate = _dot(km_hi, qh) + _dot(km_lo, qh)
                rank = jnp.zeros((per, blk), F32)
                for m in range(n_past):
                    gm = gate[m:m + 1, :]
                    rank = rank + jnp.where(blk_row > m, jnp.where(gm >= gate, 1.0, 0.0),
                                            jnp.where(gm > gate, 1.0, 0.0))
                bias = jnp.where(rank < MOBA_TOPK, 0.0, NEG_INF)
            scores.append((s, bias))

    for ip, p in enumerate(ps):
        v_all = jnp.concatenate([vt_ref[0, p, :, 0:nk], jnp.ones((2 * SUBLANES, nk), BF16)], axis=0)
        outs = []
        for hh in range(2):
            s, bias = scores[2 * ip + hh]
            pieces = [s[j * blk:(j + 1) * blk, :] for j in range(n_past)]
            if gated:
                pieces = [piece + bias[j:j + 1, :] for j, piece in enumerate(pieces)]
            pieces.append(jnp.where(causal, s[n_past * blk:, :], NEG_INF))
            m = jnp.max(pieces[0], axis=0, keepdims=True)
            for piece in pieces[1:]:
                m = jnp.maximum(m, jnp.max(piece, axis=0, keepdims=True))
            pes = [jnp.exp2(piece - m).astype(BF16) for piece in pieces]
            pe_all = jnp.concatenate(pes, axis=0) if n_past else pes[0]
            pv = _dot(v_all, pe_all)
            den = pv[LANES:LANES + 1, :]
            outs.append(pv[hh * head_dim:(hh + 1) * head_dim, :] / den)
        ot_ref[p] = jnp.concatenate(outs, axis=0)


def _attn_prompt_kernel(qt_ref, kb_ref, vt_ref, km_ref, o_ref, ot_ref, *, nb, head_dim):
    i = pl.program_id(1)
    n_pairs = ot_ref.shape[0]
    trips = n_pairs // ATTN_PAIRS_PER_TRIP
    for n in range(nb):
        @pl.when(i == n)
        def _(n=n):
            def some_pairs(p, carry):
                ps = tuple(p + r * trips for r in range(ATTN_PAIRS_PER_TRIP))
                _attn_pairs(n, ps, qt_ref, kb_ref, vt_ref, km_ref, ot_ref, head_dim)
                return carry
            lax.fori_loop(0, trips, some_pairs, 0)
    o_ref[0] = jnp.concatenate([ot_ref[p] for p in range(n_pairs)], axis=0).T.astype(o_ref.dtype)


def _attn_prompt(qt, kb, vt, kmean, head_dim):
    b_n, n_pairs, t_len, _ = kb.shape
    hd = n_pairs * LANES
    nb = t_len // MOBA_BLOCK
    per = kmean.shape[2]
    return pl.pallas_call(
        functools.partial(_attn_prompt_kernel, nb=nb, head_dim=head_dim),
        grid=(b_n, nb),
        in_specs=[
            pl.BlockSpec((1, n_pairs, LANES, MOBA_BLOCK), lambda b, i: (b, 0, 0, i)),
            pl.BlockSpec((1, n_pairs, t_len, LANES), lambda b, i: (b, 0, 0, 0)),
            pl.BlockSpec((1, n_pairs, LANES, t_len), lambda b, i: (b, 0, 0, 0)),
            pl.BlockSpec((1, n_pairs, per, LANES), lambda b, i: (b, 0, 0, 0)),
        ],
        out_specs=pl.BlockSpec((1, MOBA_BLOCK, hd), lambda b, i: (b, i, 0)),
        out_shape=jax.ShapeDtypeStruct((b_n, t_len, hd), BF16),
        scratch_shapes=[pltpu.VMEM((n_pairs, LANES, MOBA_BLOCK), F32)],
        compiler_params=_params(("arbitrary", "arbitrary")),
    )(qt, kb, vt, kmean)


def _attn_sample_kernel(pt_ref, *refs, npg, n_new, head_dim):
    del pt_ref
    ck = refs[:npg]
    cv = refs[npg:2 * npg]
    q_ref, kn_ref, vn_ref, o_ref, s_ref = refs[2 * npg:]
    page = ck[0].shape[2]
    hd = q_ref.shape[-1]
    ncol = s_ref.shape[1]
    ppb = MOBA_BLOCK // page
    n_past = npg // ppb

    r_head = lax.broadcasted_iota(jnp.int32, (ncol, hd), 0) // n_new
    c_head = lax.broadcasted_iota(jnp.int32, (ncol, hd), 1) // head_dim
    own_head = r_head == c_head
    wt = jnp.where(own_head, jnp.tile(q_ref[0], (ncol // n_new, 1)), 0.0).astype(BF16)

    for pg in range(npg):
        s_ref[pg] = _dot(wt, ck[pg][0].astype(BF16))

    gates = []
    for n in range(n_past):
        tot = jnp.sum(s_ref[n * ppb], axis=1, keepdims=True)
        for r in range(1, ppb):
            tot = tot + jnp.sum(s_ref[n * ppb + r], axis=1, keepdims=True)
        gates.append(tot * (1.0 / MOBA_BLOCK))
    bias = []
    for n in range(n_past):
        rank = jnp.zeros((ncol, 1), F32)
        for m in range(n_past):
            if m != n:
                beats = (gates[m] >= gates[n]) if m < n else (gates[m] > gates[n])
                rank = rank + jnp.where(beats, 1.0, 0.0)
        bias.append(jnp.where(rank < MOBA_TOPK, 0.0, NEG_INF))

    kn = jnp.concatenate([kn_ref[0], jnp.zeros((page - n_new, hd), F32)], axis=0)
    s_own = _dot_nt(wt, kn.astype(BF16))
    qry_t = lax.broadcasted_iota(jnp.int32, (ncol, page), 0) % n_new
    key_t = lax.broadcasted_iota(jnp.int32, (ncol, page), 1)
    s_own = jnp.where(key_t <= qry_t, s_own, NEG_INF)

    mx = jnp.max(s_own, axis=1, keepdims=True)
    for pg in range(npg):
        mx = jnp.maximum(mx, jnp.max(s_ref[pg], axis=1, keepdims=True) + bias[pg // ppb])
    p_own = jnp.exp2(s_own - mx)
    den = jnp.sum(p_own, axis=1, keepdims=True)
    for pg in range(npg):
        pe = jnp.exp2(s_ref[pg] + (bias[pg // ppb] - mx))
        s_ref[pg] = pe
        den = den + jnp.sum(pe, axis=1, keepdims=True)
    vn = jnp.concatenate([vn_ref[0], jnp.zeros((page - n_new, hd), F32)], axis=0)
    acc = _dot(p_own.astype(BF16), vn.astype(BF16))
    for pg in range(npg):
        acc = acc + _dot_nt(s_ref[pg].astype(BF16), cv[pg][0].astype(BF16))
    diag = jnp.where(own_head, acc / den, 0.0)
    out = diag[0:n_new, :]
    for h in range(1, ncol // n_new):
        out = out + diag[h * n_new:(h + 1) * n_new, :]
    o_ref[0] = out


def _attn_sample(page_table, cache_kt, cache_vt, q, k_new, v_new, head_dim):
    bs, n_new, hd = q.shape
    npg = page_table.shape[1]
    page = cache_kt.shape[2]
    ncol = (hd // head_dim) * n_new
    assert (npg * page) % MOBA_BLOCK == 0 and MOBA_BLOCK % page == 0 and n_new <= page
    assert ncol % SUBLANES == 0 and n_new % SUBLANES == 0

    def page_spec(r):
        return pl.BlockSpec((1, hd, page), lambda b, pt: (pt[b, r], 0, 0))

    seq_spec = pl.BlockSpec((1, n_new, hd), lambda b, pt: (b, 0, 0))
    grid_spec = pltpu.PrefetchScalarGridSpec(
        num_scalar_prefetch=1,
        grid=(bs,),
        in_specs=[page_spec(r) for r in range(npg)] * 2 + [seq_spec, seq_spec, seq_spec],
        out_specs=seq_spec,
        scratch_shapes=[pltpu.VMEM((npg, ncol, page), F32)],
    )
    return pl.pallas_call(
        functools.partial(_attn_sample_kernel, npg=npg, n_new=n_new, head_dim=head_dim),
        grid_spec=grid_spec,
        out_shape=jax.ShapeDtypeStruct((bs, n_new, hd), F32),
        compiler_params=_params(("arbitrary",)),
    )(page_table, *([cache_kt] * npg), *([cache_vt] * npg), q, k_new, v_new)


def kernel(x_prompt, x_sample, state_ssm_re, state_ssm_im, cache_k, cache_v, page_table,
           ssm_norm, ssm_lambda_re, ssm_lambda_im, ssm_log_dt, ssm_b_re, ssm_b_im,
           ssm_c_re, ssm_c_im, ssm_d, ssm_w_glu, kv_norm, w_kv, k_norm,
           attn_norm, w_q, q_norm, w_o, mlp_norm, w_up, w_down):
    b_p, t_p, d_model = x_prompt.shape
    b_s, t_s, _ = x_sample.shape
    n_pool, page, n_heads, head_dim = cache_k.shape
    hd = n_heads * head_dim
    g_n, p_n = ssm_lambda_re.shape[1:]
    gp = g_n * p_n
    past_len = page_table.shape[1] * page
    n_p, n_s = t_p * b_p, t_s * b_s
    assert ssm_norm.shape[0] == 1 and attn_norm.shape[0] == 1, "one S5 layer then one MoBA layer"
    assert b_p % SUBLANES == 0 and b_s % SUBLANES == 0 and t_p % MOBA_BLOCK == 0
    assert g_n % SSM_CHUNK_GROUPS == 0 and gp % SSM_SCAN_LANES == 0 and hd % LANES == 0
    assert t_p % SSM_TIME_TILE == 0 and n_p % MLP_ROWS == 0 and n_s % MLP_ROWS == 0
    assert SAMPLE_KVQ_ROWS % t_s == 0 and n_s % SAMPLE_KVQ_ROWS == 0

    row = lambda w: w.reshape(1, -1)
    a_b, bmat, cmat = _ssm_weights(ssm_lambda_re[0], ssm_lambda_im[0], ssm_log_dt[0],
                                   ssm_b_re[0], ssm_b_im[0], ssm_c_re[0], ssm_c_im[0])
    wglu = ssm_w_glu[0].astype(BF16)
    wup = w_up.astype(BF16)
    wdn = w_down.astype(BF16)
    wkv = w_kv.astype(BF16)
    wq = w_q[0].astype(BF16)
    wo = w_o[0].astype(BF16)
    head_of = jnp.arange(hd) // head_dim
    e_mat = (head_of[:, None] == head_of[None, :]).astype(BF16)
    kn_t = row(jnp.tile(k_norm, n_heads))
    qn_t = row(jnp.tile(q_norm[0], n_heads))

    h0_p = jnp.zeros((b_p // SUBLANES, SUBLANES, 2 * gp), F32)
    h0_s = jnp.concatenate([state_ssm_re[0].reshape(b_s, gp), state_ssm_im[0].reshape(b_s, gp)],
                           axis=1).reshape(b_s // SUBLANES, SUBLANES, 2 * gp)
    ssm_args = (row(ssm_norm[0]), a_b, bmat, cmat, row(ssm_d[0]), wglu)
    hp, fin_p = _ssm_layer(x_prompt, h0_p, *ssm_args, tt=SSM_TIME_TILE)
    hs, fin_s = _ssm_layer(x_sample, h0_s, *ssm_args, tt=t_s)

    def split_state(fin, b_n):
        fin = fin.reshape(b_n, 2 * gp)
        return fin[:, :gp].reshape(1, b_n, g_n, p_n), fin[:, gp:].reshape(1, b_n, g_n, p_n)

    ssm_re_p, ssm_im_p = split_state(fin_p, b_p)
    ssm_re_s, ssm_im_s = split_state(fin_s, b_s)

    mlp0 = (row(mlp_norm[0]), wup[0], wdn[0])
    hp = _mlp(hp.reshape(n_p, d_model), *mlp0)
    hs = _mlp(hs.reshape(n_s, d_model), *mlp0)

    nb_p = t_p // MOBA_BLOCK
    cos_p, sin_p = _rope_cos_sin(jnp.arange(t_p, dtype=jnp.int32), head_dim)
    rope_pt = jnp.concatenate([cos_p.T, sin_p.T], axis=0)
    lanes_of = lambda g: jnp.broadcast_to(g[:, None], (head_dim, MOBA_BLOCK))
    k_p, v_p, kb_p, qt_p, vt_p, km_p = _kvq_prompt(
        hp, rope_pt, b_p, row(kv_norm), row(attn_norm[0]), wkv.T, wq.T,
        lanes_of(k_norm), lanes_of(q_norm[0] * (head_dim ** -0.5 * LOG2E)), head_dim)
    pos_s = past_len + jnp.arange(t_s, dtype=jnp.int32)
    rope_s = jnp.tile(_rope_table(pos_s, head_dim), (SAMPLE_KVQ_ROWS // t_s, 1))
    k_s, v_s, q_s = _kvq_sample(hs, rope_s, SAMPLE_KVQ_ROWS, row(kv_norm), row(attn_norm[0]),
                                wkv, wq, kn_t, qn_t, e_mat, head_dim)

    n_pairs = hd // LANES
    km_p = jnp.transpose(km_p.reshape(b_p, nb_p, n_pairs, LANES), (0, 2, 1, 3))
    km_p = jnp.pad(km_p, ((0, 0), (0, 0), (0, -nb_p % SUBLANES), (0, 0)))
    o_p = _attn_prompt(qt_p.reshape(b_p, n_pairs, LANES, t_p), kb_p,
                       vt_p.reshape(b_p, n_pairs, LANES, t_p), km_p, head_dim)
    cache_kt = jnp.transpose(cache_k, (0, 2, 3, 1)).reshape(n_pool, hd, page)
    cache_vt = jnp.transpose(cache_v, (0, 2, 3, 1)).reshape(n_pool, hd, page)
    o_s = _attn_sample(page_table, cache_kt, cache_vt, q_s.reshape(b_s, t_s, hd),
                       k_s.reshape(b_s, t_s, hd), v_s.reshape(b_s, t_s, hd), head_dim)

    mlp1 = (row(mlp_norm[1]), wup[1], wdn[1])
    y_p = _mlp(hp, *mlp1, o=o_p.reshape(n_p, hd), wo=wo)
    y_s = _mlp(hs, *mlp1, o=o_s.reshape(n_s, hd), wo=wo)

    to_bthd = lambda a: jnp.transpose(a.reshape(b_p, n_heads, head_dim, t_p), (0, 3, 1, 2))
    return (y_p.reshape(b_p, t_p, d_model), y_s.reshape(b_s, t_s, d_model),
            to_bthd(k_p), to_bthd(v_p),
            k_s.reshape(b_s, t_s, n_heads, head_dim), v_s.reshape(b_s, t_s, n_heads, head_dim),
            ssm_re_p, ssm_im_p, ssm_re_s, ssm_im_s)
```

```python
import functools

import jax
import jax.numpy as jnp
from jax import lax
from jax.experimental import pallas as pl
from jax.experimental.pallas import tpu as pltpu

F32 = jnp.float32
BF16 = jnp.bfloat16

EPS = 1e-6
NEG_INF = -1e30
ROPE_THETA = 500000.0
MOBA_BLOCK = 256
MOBA_TOPK = 3
LOG2E = 1.4426950408889634

SUBLANES = 8
LANES = 128
VMEM_LIMIT = 56 * 1024 * 1024

SSM_CHUNK_GROUPS = 8
SSM_SCAN_LANES = 512
SSM_TIME_TILE = 64
MLP_ROWS = 512
SAMPLE_KVQ_ROWS = 128
ATTN_PAIRS_PER_TRIP = 4


def _const_spec(shape):
    nd = len(shape)
    return pl.BlockSpec(shape, lambda *_: (0,) * nd, pipeline_mode=pl.Buffered(1))


def _params(sem):
    return pltpu.CompilerParams(dimension_semantics=sem, vmem_limit_bytes=VMEM_LIMIT)


def _rms(x):
    return x * lax.rsqrt(jnp.mean(x * x, axis=-1, keepdims=True) + EPS)


def _dot(a, b):
    return jnp.dot(a, b, preferred_element_type=F32)


def _dot_nt(a, b):
    return lax.dot_general(a, b, (((1,), (1,)), ((), ())), preferred_element_type=F32)


def _split_bf16(x):
    hi = x.astype(BF16)
    lo = (x - hi.astype(F32)).astype(BF16)
    return hi, lo


def _ssm_kernel(x_ref, h0_ref, g_ref, a_ref, bmat_ref, cmat_ref, d_ref, wglu_ref,
                out_ref, hfin_ref, s_ref, hst_ref, xs_ref, os_ref, *, tt, gp, nkc, kc, nc):
    ti = pl.program_id(1)
    d_model = x_ref.shape[-1]
    rows = tt * SUBLANES

    @pl.when(ti == 0)
    def _():
        hst_ref[...] = h0_ref[0]

    n_cb = d_model // LANES
    for b in range(SUBLANES):
        xb = x_ref[b]
        for c in range(n_cb):
            xs_ref[c, pl.ds(b, tt, stride=SUBLANES), :] = xb[:, c * LANES:(c + 1) * LANES]
    x = jnp.concatenate([xs_ref[c] for c in range(n_cb)], axis=1)
    u = _rms(x) * g_ref[...]
    ub = u.astype(BF16)

    for c in range(nkc):
        res = _dot(ub[:, c * kc:(c + 1) * kc], bmat_ref[c])
        s_ref[:, :, c * nc:(c + 1) * nc] = res[:, :nc].reshape(tt, SUBLANES, nc)
        s_ref[:, :, gp + c * nc:gp + (c + 1) * nc] = res[:, nc:].reshape(tt, SUBLANES, nc)

    w = SSM_SCAN_LANES
    for c in range(gp // w):
        re = slice(c * w, (c + 1) * w)
        im = slice(gp + c * w, gp + (c + 1) * w)
        ar = a_ref[:, re]
        ai = a_ref[:, im]

        def step(t, carry, re=re, im=im, ar=ar, ai=ai):
            hr, hi = carry
            nr = ar * hr - ai * hi + s_ref[t, :, re]
            ni = ar * hi + ai * hr + s_ref[t, :, im]
            s_ref[t, :, re] = nr
            s_ref[t, :, im] = ni
            return nr, ni

        hr, hi = lax.fori_loop(0, tt, step, (hst_ref[:, re], hst_ref[:, im]),
                               unroll=min(tt, 8))
        hst_ref[:, re] = hr
        hst_ref[:, im] = hi

    ys = []
    for c in range(nkc):
        sr = s_ref[:, :, c * nc:(c + 1) * nc].reshape(rows, nc).astype(BF16)
        si = s_ref[:, :, gp + c * nc:gp + (c + 1) * nc].reshape(rows, nc).astype(BF16)
        ys.append(_dot(sr, cmat_ref[c, :nc, :]) + _dot(si, cmat_ref[c, nc:, :]))
    y = jnp.concatenate(ys, axis=1) + d_ref[...] * u
    y = jax.nn.gelu(y, approximate=True)
    z = _dot(y.astype(BF16), wglu_ref[...])
    out = x + z[:, :d_model] * jax.nn.sigmoid(z[:, d_model:])
    for c in range(n_cb):
        os_ref[c] = out[:, c * LANES:(c + 1) * LANES]
    for b in range(SUBLANES):
        for c in range(n_cb):
            out_ref[b, :, c * LANES:(c + 1) * LANES] = os_ref[c, pl.ds(b, tt, stride=SUBLANES), :]

    @pl.when(ti == pl.num_programs(1) - 1)
    def _():
        hfin_ref[0] = hst_ref[...]


def _ssm_layer(x, h0, g, a_b, bmat, cmat, d, wglu, tt):
    b_tot, t_len, d_model = x.shape
    nbg = b_tot // SUBLANES
    gp2 = h0.shape[-1]
    gp = gp2 // 2
    nkc, kc, nc2 = bmat.shape
    kern = functools.partial(_ssm_kernel, tt=tt, gp=gp, nkc=nkc, kc=kc, nc=nc2 // 2)
    return pl.pallas_call(
        kern,
        grid=(nbg, t_len // tt),
        in_specs=[
            pl.BlockSpec((SUBLANES, tt, d_model), lambda b, t: (b, t, 0)),
            pl.BlockSpec((1, SUBLANES, gp2), lambda b, t: (b, 0, 0)),
            _const_spec(g.shape), _const_spec(a_b.shape), _const_spec(bmat.shape),
            _const_spec(cmat.shape), _const_spec(d.shape), _const_spec(wglu.shape),
        ],
        out_specs=[
            pl.BlockSpec((SUBLANES, tt, d_model), lambda b, t: (b, t, 0)),
            pl.BlockSpec((1, SUBLANES, gp2), lambda b, t: (b, 0, 0)),
        ],
        out_shape=[
            jax.ShapeDtypeStruct(x.shape, F32),
            jax.ShapeDtypeStruct(h0.shape, F32),
        ],
        scratch_shapes=[
            pltpu.VMEM((tt, SUBLANES, gp2), F32),
            pltpu.VMEM((SUBLANES, gp2), F32),
            pltpu.VMEM((d_model // LANES, tt * SUBLANES, LANES), F32),
            pltpu.VMEM((d_model // LANES, tt * SUBLANES, LANES), F32),
        ],
        compiler_params=_params(("arbitrary", "arbitrary")),
    )(x, h0, g, a_b, bmat, cmat, d, wglu)


def _ssm_weights(lam_re, lam_im, log_dt, b_re, b_im, c_re, c_im):
    g_n, p_n, c_n = b_re.shape
    dt = jnp.exp(log_dt)[:, None]
    mag = jnp.exp(lam_re * dt)
    ab_re, ab_im = mag * jnp.cos(lam_im * dt), mag * jnp.sin(lam_im * dt)
    nr, ni = ab_re - 1.0, ab_im
    den = lam_re * lam_re + lam_im * lam_im
    f_re = (nr * lam_re + ni * lam_im) / den
    f_im = (ni * lam_re - nr * lam_im) / den
    bb_re = f_re[..., None] * b_re - f_im[..., None] * b_im
    bb_im = f_re[..., None] * b_im + f_im[..., None] * b_re
    gk = SSM_CHUNK_GROUPS
    nkc = g_n // gk
    eye = jnp.eye(gk, dtype=F32)

    def bdiag_in(w):
        w = w.reshape(nkc, gk, p_n, c_n)
        return jnp.einsum('kgpc,gh->kgchp', w, eye).reshape(nkc, gk * c_n, gk * p_n)

    def bdiag_out(w):
        w = w.reshape(nkc, gk, c_n, p_n)
        return jnp.einsum('kgcp,gh->kgphc', w, eye).reshape(nkc, gk * p_n, gk * c_n)

    bmat = jnp.concatenate([bdiag_in(bb_re), bdiag_in(bb_im)], axis=2).astype(BF16)
    cmat = jnp.concatenate([bdiag_out(c_re), bdiag_out(-c_im)], axis=1).astype(BF16)
    a_row = jnp.concatenate([ab_re.reshape(-1), ab_im.reshape(-1)])
    a_b = jnp.broadcast_to(a_row[None, :], (SUBLANES, a_row.shape[0]))
    return a_b, bmat, cmat


def _mlp_kernel(*refs, pre):
    if pre:
        h_ref, o_ref, wo_ref, g_ref, wup_ref, wdn_ref, out_ref = refs
        h = h_ref[...] + _dot(o_ref[...].astype(BF16), wo_ref[...])
    else:
        h_ref, g_ref, wup_ref, wdn_ref, out_ref = refs
        h = h_ref[...]
    u = (_rms(h) * g_ref[...]).astype(BF16)
    a = jnp.square(jnp.maximum(_dot(u, wup_ref[...]), 0.0))
    out_ref[...] = h + _dot(a.astype(BF16), wdn_ref[...])


def _mlp(h, g, wup, wdn, o=None, wo=None):
    n_rows, d_model = h.shape
    blk = pl.BlockSpec((MLP_ROWS, d_model), lambda i: (i, 0))
    pre = o is not None
    in_specs = [blk]
    args = [h]
    if pre:
        in_specs += [pl.BlockSpec((MLP_ROWS, o.shape[-1]), lambda i: (i, 0)), _const_spec(wo.shape)]
        args += [o, wo]
    in_specs += [_const_spec(g.shape), _const_spec(wup.shape), _const_spec(wdn.shape)]
    args += [g, wup, wdn]
    return pl.pallas_call(
        functools.partial(_mlp_kernel, pre=pre),
        grid=(n_rows // MLP_ROWS,),
        in_specs=in_specs,
        out_specs=blk,
        out_shape=jax.ShapeDtypeStruct((n_rows, d_model), F32),
        compiler_params=_params(("arbitrary",)),
    )(*args)


def _kvq_kernel(h_ref, rope_ref, gkv_ref, gq_ref, wkv_ref, wq_ref, kn_ref, qn_ref, e_ref,
                k_ref, v_ref, q_ref, *, head_dim):
    hd = wq_ref.shape[-1]
    half = head_dim // 8
    hn = _rms(h_ref[...])
    kv = _dot((hn * gkv_ref[...]).astype(BF16), wkv_ref[...])
    q = _dot((hn * gq_ref[...]).astype(BF16), wq_ref[...])
    reps = hd // LANES
    rope = rope_ref[...]
    cos_t = jnp.tile(rope[:, :LANES], (1, reps))
    sin_lo = jnp.tile(rope[:, LANES:2 * LANES], (1, reps))
    sin_hi = jnp.tile(rope[:, 2 * LANES:], (1, reps))

    def headnorm_rope(x, gain):
        hi, lo = _split_bf16(x * x)
        ss = _dot(hi, e_ref[...]) + _dot(lo, e_ref[...])
        xn = x * lax.rsqrt(ss * (1.0 / head_dim) + EPS) * gain
        return (xn * cos_t + pltpu.roll(xn, hd - half, 1) * sin_lo
                + pltpu.roll(xn, half, 1) * sin_hi)

    k_ref[...] = headnorm_rope(kv[:, :hd], kn_ref[...])
    v_ref[...] = kv[:, hd:]
    q_ref[...] = headnorm_rope(q, qn_ref[...]) * (head_dim ** -0.5 * LOG2E)


def _kvq_sample(h, rope, rows, gkv, gq, wkv, wq, kn, qn, e_mat, head_dim):
    n_rows, d_model = h.shape
    hd = wq.shape[-1]
    row_map = lambda i: (i, 0)
    out = jax.ShapeDtypeStruct((n_rows, hd), F32)
    return pl.pallas_call(
        functools.partial(_kvq_kernel, head_dim=head_dim),
        grid=(n_rows // rows,),
        in_specs=[
            pl.BlockSpec((rows, d_model), row_map),
            _const_spec(rope.shape),
            _const_spec(gkv.shape), _const_spec(gq.shape), _const_spec(wkv.shape),
            _const_spec(wq.shape), _const_spec(kn.shape), _const_spec(qn.shape),
            _const_spec(e_mat.shape),
        ],
        out_specs=[pl.BlockSpec((rows, hd), row_map)] * 3,
        out_shape=[out, out, out],
        compiler_params=_params(("arbitrary",)),
    )(h, rope, gkv, gq, wkv, wq, kn, qn, e_mat)


def _kvq_prompt_kernel(h_ref, rope_ref, gkv_ref, gq_ref, wkv_ref, wq_ref, kn_ref, qn_ref,
                       k_ref, v_ref, kb_ref, qt_ref, vt_ref, km_ref, *, head_dim):
    hd = wq_ref.shape[0]
    rows = h_ref.shape[0]
    n_heads = hd // head_dim
    half = head_dim // 8
    hn = _rms(h_ref[...])
    kvt = _dot_nt(wkv_ref[...], (hn * gkv_ref[...]).astype(BF16))
    qt = _dot_nt(wq_ref[...], (hn * gq_ref[...]).astype(BF16))
    cos = rope_ref[0:half, :][None]
    sin = rope_ref[half:2 * half, :][None]

    def headnorm_rope_t(xt, gain):
        x3 = xt.reshape(n_heads, head_dim, rows)
        ss = jnp.sum(x3 * x3, axis=1, keepdims=True)
        xn = x3 * lax.rsqrt(ss * (1.0 / head_dim) + EPS) * gain[None]
        lo, hi = xn[:, 0:half, :], xn[:, half:2 * half, :]
        out = jnp.concatenate([lo * cos - hi * sin, hi * cos + lo * sin, xn[:, 2 * half:, :]], axis=1)
        return out.reshape(hd, rows)

    kt = headnorm_rope_t(kvt[:hd, :], kn_ref[...])
    vt = kvt[hd:, :]
    k_ref[0] = kt
    v_ref[0] = vt
    qt_ref[0] = headnorm_rope_t(qt, qn_ref[...]).astype(BF16)
    vt_ref[0] = vt.astype(BF16)
    k = kt.T
    km_ref[0] = jnp.mean(k, axis=0, keepdims=True)
    kb = k.astype(BF16)
    for p in range(hd // LANES):
        kb_ref[0, p] = kb[:, p * LANES:(p + 1) * LANES]


def _kvq_prompt(h, rope_t, n_seq, gkv, gq, wkv, wq, kn_b, qn_b, head_dim):
    n_rows, d_model = h.shape
    hd = wq.shape[0]
    rows = MOBA_BLOCK
    t_len = n_rows // n_seq
    nb = t_len // rows
    t_spec = pl.BlockSpec((1, hd, rows), lambda i: (i // nb, 0, i % nb))
    t_f32 = jax.ShapeDtypeStruct((n_seq, hd, t_len), F32)
    t_bf16 = jax.ShapeDtypeStruct((n_seq, hd, t_len), BF16)
    return pl.pallas_call(
        functools.partial(_kvq_prompt_kernel, head_dim=head_dim),
        grid=(n_seq * nb,),
        in_specs=[
            pl.BlockSpec((rows, d_model), lambda i: (i, 0)),
            pl.BlockSpec((rope_t.shape[0], rows), lambda i: (0, i % nb)),
            _const_spec(gkv.shape), _const_spec(gq.shape), _const_spec(wkv.shape),
            _const_spec(wq.shape), _const_spec(kn_b.shape), _const_spec(qn_b.shape),
        ],
        out_specs=[t_spec, t_spec,
                   pl.BlockSpec((1, hd // LANES, rows, LANES), lambda i: (i // nb, 0, i % nb, 0)),
                   t_spec, t_spec,
                   pl.BlockSpec((1, 1, hd), lambda i: (i, 0, 0))],
        out_shape=[t_f32, t_f32,
                   jax.ShapeDtypeStruct((n_seq, hd // LANES, t_len, LANES), BF16),
                   t_bf16, t_bf16,
                   jax.ShapeDtypeStruct((n_seq * nb, 1, hd), F32)],
        compiler_params=_params(("arbitrary",)),
    )(h, rope_t, gkv, gq, wkv, wq, kn_b, qn_b)


def _rope_cos_sin(pos, head_dim):
    half = head_dim // 8
    inv = ROPE_THETA ** (-jnp.arange(half, dtype=F32) / half)
    ang = pos.astype(F32)[:, None] * inv[None, :]
    return jnp.cos(ang), jnp.sin(ang)


def _rope_table(pos, head_dim):
    rot = head_dim // 4
    half = rot // 2
    cos, sin = _rope_cos_sin(pos, head_dim)
    j = jnp.arange(LANES) % head_dim
    cos_t = jnp.where(j[None, :] < rot, cos[:, j % half], 1.0)
    sin_lo = jnp.where(j[None, :] < half, -sin[:, j % half], 0.0)
    sin_hi = jnp.where((j[None, :] >= half) & (j[None, :] < rot), sin[:, j % half], 0.0)
    return jnp.concatenate([cos_t, sin_lo, sin_hi], axis=1)


def _attn_pairs(n_past, ps, qt_ref, kb_ref, vt_ref, km_ref, ot_ref, head_dim):
    blk = MOBA_BLOCK
    per = km_ref.shape[2]
    nk = (n_past + 1) * blk
    low = lax.broadcasted_iota(jnp.int32, (LANES, 1), 0) < head_dim
    causal = (lax.broadcasted_iota(jnp.int32, (blk, blk), 0)
              <= lax.broadcasted_iota(jnp.int32, (blk, blk), 1))
    gated = n_past > MOBA_TOPK
    blk_row = lax.broadcasted_iota(jnp.int32, (per, blk), 0)

    scores = []
    for p in ps:
        qt = qt_ref[0, p]
        zq = jnp.zeros_like(qt)
        k_all = kb_ref[0, p, 0:nk, :]
        for qh in (jnp.where(low, qt, zq), jnp.where(low, zq, qt)):
            s = _dot(k_all, qh)
            bias = None
            if gated:
                km_hi, km_lo = _split_bf16(km_ref[0, p])
                gate = _dot(km_hi, qh) + _dot(km_lo, qh)
                rank = jnp.zeros((per, blk), F32)
                for m in range(n_past):
                    gm = gate[m:m + 1, :]
                    rank = rank + jnp.where(blk_row > m, jnp.where(gm >= gate, 1.0, 0.0),
                                            jnp.where(gm > gate, 1.0, 0.0))
                bias = jnp.where(rank < MOBA_TOPK, 0.0, NEG_INF)
            scores.append((s, bias))

    for ip, p in enumerate(ps):
        v_all = jnp.concatenate([vt_ref[0, p, :, 0:nk], jnp.ones((2 * SUBLANES, nk), BF16)], axis=0)
        outs = []
        for hh in range(2):
            s, bias = scores[2 * ip + hh]
            pieces = [s[j * blk:(j + 1) * blk, :] for j in range(n_past)]
            if gated:
                pieces = [piece + bias[j:j + 1, :] for j, piece in enumerate(pieces)]
            pieces.append(jnp.where(causal, s[n_past * blk:, :], NEG_INF))
            m = jnp.max(pieces[0], axis=0, keepdims=True)
            for piece in pieces[1:]:
                m = jnp.maximum(m, jnp.max(piece, axis=0, keepdims=True))
            pes = [jnp.exp2(piece - m).astype(BF16) for piece in pieces]
            pe_all = jnp.concatenate(pes, axis=0) if n_past else pes[0]
            pv = _dot(v_all, pe_all)
            den = pv[LANES:LANES + 1, :]
            outs.append(pv[hh * head_dim:(hh + 1) * head_dim, :] / den)
        ot_ref[p] = jnp.concatenate(outs, axis=0)


def _attn_prompt_kernel(qt_ref, kb_ref, vt_ref, km_ref, o_ref, ot_ref, *, nb, head_dim):
    i = pl.program_id(1)
    n_pairs = ot_ref.shape[0]
    trips = n_pairs // ATTN_PAIRS_PER_TRIP
    for n in range(nb):
        @pl.when(i == n)
        def _(n=n):
            def some_pairs(p, carry):
                ps = tuple(p + r * trips for r in range(ATTN_PAIRS_PER_TRIP))
                _attn_pairs(n, ps, qt_ref, kb_ref, vt_ref, km_ref, ot_ref, head_dim)
                return carry
            lax.fori_loop(0, trips, some_pairs, 0)
    o_ref[0] = jnp.concatenate([ot_ref[p] for p in range(n_pairs)], axis=0).T.astype(o_ref.dtype)


def _attn_prompt(qt, kb, vt, kmean, head_dim):
    b_n, n_pairs, t_len, _ = kb.shape
    hd = n_pairs * LANES
    nb = t_len // MOBA_BLOCK
    per = kmean.shape[2]
    return pl.pallas_call(
        functools.partial(_attn_prompt_kernel, nb=nb, head_dim=head_dim),
        grid=(b_n, nb),
        in_specs=[
            pl.BlockSpec((1, n_pairs, LANES, MOBA_BLOCK), lambda b, i: (b, 0, 0, i)),
            pl.BlockSpec((1, n_pairs, t_len, LANES), lambda b, i: (b, 0, 0, 0)),
            pl.BlockSpec((1, n_pairs, LANES, t_len), lambda b, i: (b, 0, 0, 0)),
            pl.BlockSpec((1, n_pairs, per, LANES), lambda b, i: (b, 0, 0, 0)),
        ],
        out_specs=pl.BlockSpec((1, MOBA_BLOCK, hd), lambda b, i: (b, i, 0)),
        out_shape=jax.ShapeDtypeStruct((b_n, t_len, hd), BF16),
        scratch_shapes=[pltpu.VMEM((n_pairs, LANES, MOBA_BLOCK), F32)],
        compiler_params=_params(("arbitrary", "arbitrary")),
    )(qt, kb, vt, kmean)


def _attn_sample_kernel(pt_ref, *refs, npg, n_new, head_dim):
    del pt_ref
    ck = refs[:npg]
    cv = refs[npg:2 * npg]
    q_ref, kn_ref, vn_ref, o_ref, s_ref, p_ref, acc_ref, den_ref = refs[2 * npg:]
    page = ck[0].shape[2]
    hd = q_ref.shape[-1]
    ncol = s_ref.shape[1]
    ppb = MOBA_BLOCK // page
    n_past = npg // ppb

    @pl.when(pl.program_id(0) == 0)
    def _():
        p_ref[...] = jnp.zeros_like(p_ref)
        acc_ref[...] = jnp.zeros_like(acc_ref)
        den_ref[...] = jnp.ones_like(den_ref)

    r_head = lax.broadcasted_iota(jnp.int32, (ncol, hd), 0) // n_new
    c_head = lax.broadcasted_iota(jnp.int32, (ncol, hd), 1) // head_dim
    own_head = r_head == c_head
    wt = jnp.where(own_head, jnp.tile(q_ref[0], (ncol // n_new, 1)), 0.0).astype(BF16)

    for pg in range(npg):
        s_ref[pg] = _dot(wt, ck[pg][0].astype(BF16))

    acc = acc_ref[...]
    for pg in range(npg):
        acc = acc + _dot_nt(p_ref[pg], cv[pg][0].astype(BF16))
    diag = jnp.where(own_head, acc / den_ref[...], 0.0)
    out = diag[0:n_new, :]
    for h in range(1, ncol // n_new):
        out = out + diag[h * n_new:(h + 1) * n_new, :]
    o_ref[0] = out

    gates = []
    for n in range(n_past):
        tot = jnp.sum(s_ref[n * ppb], axis=1, keepdims=True)
        for r in range(1, ppb):
            tot = tot + jnp.sum(s_ref[n * ppb + r], axis=1, keepdims=True)
        gates.append(tot * (1.0 / MOBA_BLOCK))
    bias = []
    for n in range(n_past):
        rank = jnp.zeros((ncol, 1), F32)
        for m in range(n_past):
            if m != n:
                beats = (gates[m] >= gates[n]) if m < n else (gates[m] > gates[n])
                rank = rank + jnp.where(beats, 1.0, 0.0)
        bias.append(jnp.where(rank < MOBA_TOPK, 0.0, NEG_INF))

    kn = jnp.concatenate([kn_ref[0], jnp.zeros((page - n_new, hd), F32)], axis=0)
    s_own = _dot_nt(wt, kn.astype(BF16))
    qry_t = lax.broadcasted_iota(jnp.int32, (ncol, page), 0) % n_new
    key_t = lax.broadcasted_iota(jnp.int32, (ncol, page), 1)
    s_own = jnp.where(key_t <= qry_t, s_own, NEG_INF)

    mx = jnp.max(s_own, axis=1, keepdims=True)
    for pg in range(npg):
        mx = jnp.maximum(mx, jnp.max(s_ref[pg], axis=1, keepdims=True) + bias[pg // ppb])
    p_own = jnp.exp2(s_own - mx)
    den = jnp.sum(p_own, axis=1, keepdims=True)
    for pg in range(npg):
        pe = jnp.exp2(s_ref[pg] + (bias[pg // ppb] - mx))
        p_ref[pg] = pe.astype(BF16)
        den = den + jnp.sum(pe, axis=1, keepdims=True)
    den_ref[...] = den
    vn = jnp.concatenate([vn_ref[0], jnp.zeros((page - n_new, hd), F32)], axis=0)
    acc_ref[...] = _dot(p_own.astype(BF16), vn.astype(BF16))


def _attn_sample(page_table, cache_kt, cache_vt, q, k_new, v_new, head_dim):
    bs, n_new, hd = q.shape
    npg = page_table.shape[1]
    page = cache_kt.shape[2]
    ncol = (hd // head_dim) * n_new
    assert (npg * page) % MOBA_BLOCK == 0 and MOBA_BLOCK % page == 0 and n_new <= page
    assert ncol % SUBLANES == 0 and n_new % SUBLANES == 0

    cur = lambda b: jnp.minimum(b, bs - 1)
    prev = lambda b: jnp.maximum(b - 1, 0)

    def page_spec(r, seq):
        return pl.BlockSpec((1, hd, page), lambda b, pt: (pt[seq(b), r], 0, 0))

    seq_spec = pl.BlockSpec((1, n_new, hd), lambda b, pt: (cur(b), 0, 0))
    grid_spec = pltpu.PrefetchScalarGridSpec(
        num_scalar_prefetch=1,
        grid=(bs + 1,),
        in_specs=([page_spec(r, cur) for r in range(npg)] + [page_spec(r, prev) for r in range(npg)]
                  + [seq_spec, seq_spec, seq_spec]),
        out_specs=pl.BlockSpec((1, n_new, hd), lambda b, pt: (prev(b), 0, 0)),
        scratch_shapes=[
            pltpu.VMEM((npg, ncol, page), F32),
            pltpu.VMEM((npg, ncol, page), BF16),
            pltpu.VMEM((ncol, hd), F32),
            pltpu.VMEM((ncol, 1), F32),
        ],
    )
    return pl.pallas_call(
        functools.partial(_attn_sample_kernel, npg=npg, n_new=n_new, head_dim=head_dim),
        grid_spec=grid_spec,
        out_shape=jax.ShapeDtypeStruct((bs, n_new, hd), F32),
        compiler_params=_params(("arbitrary",)),
    )(page_table, *([cache_kt] * npg), *([cache_vt] * npg), q, k_new, v_new)


def kernel(x_prompt, x_sample, state_ssm_re, state_ssm_im, cache_k, cache_v, page_table,
           ssm_norm, ssm_lambda_re, ssm_lambda_im, ssm_log_dt, ssm_b_re, ssm_b_im,
           ssm_c_re, ssm_c_im, ssm_d, ssm_w_glu, kv_norm, w_kv, k_norm,
           attn_norm, w_q, q_norm, w_o, mlp_norm, w_up, w_down):
    b_p, t_p, d_model = x_prompt.shape
    b_s, t_s, _ = x_sample.shape
    n_pool, page, n_heads, head_dim = cache_k.shape
    hd = n_heads * head_dim
    g_n, p_n = ssm_lambda_re.shape[1:]
    gp = g_n * p_n
    past_len = page_table.shape[1] * page
    n_p, n_s = t_p * b_p, t_s * b_s
    assert ssm_norm.shape[0] == 1 and attn_norm.shape[0] == 1, "one S5 layer then one MoBA layer"
    assert b_p % SUBLANES == 0 and b_s % SUBLANES == 0 and t_p % MOBA_BLOCK == 0
    assert g_n % SSM_CHUNK_GROUPS == 0 and gp % SSM_SCAN_LANES == 0 and hd % LANES == 0
    assert t_p % SSM_TIME_TILE == 0 and n_p % MLP_ROWS == 0 and n_s % MLP_ROWS == 0
    assert SAMPLE_KVQ_ROWS % t_s == 0 and n_s % SAMPLE_KVQ_ROWS == 0

    row = lambda w: w.reshape(1, -1)
    a_b, bmat, cmat = _ssm_weights(ssm_lambda_re[0], ssm_lambda_im[0], ssm_log_dt[0],
                                   ssm_b_re[0], ssm_b_im[0], ssm_c_re[0], ssm_c_im[0])
    wglu = ssm_w_glu[0].astype(BF16)
    wup = w_up.astype(BF16)
    wdn = w_down.astype(BF16)
    wkv = w_kv.astype(BF16)
    wq = w_q[0].astype(BF16)
    wo = w_o[0].astype(BF16)
    head_of = jnp.arange(hd) // head_dim
    e_mat = (head_of[:, None] == head_of[None, :]).astype(BF16)
    kn_t = row(jnp.tile(k_norm, n_heads))
    qn_t = row(jnp.tile(q_norm[0], n_heads))

    h0_p = jnp.zeros((b_p // SUBLANES, SUBLANES, 2 * gp), F32)
    h0_s = jnp.concatenate([state_ssm_re[0].reshape(b_s, gp), state_ssm_im[0].reshape(b_s, gp)],
                           axis=1).reshape(b_s // SUBLANES, SUBLANES, 2 * gp)
    ssm_args = (row(ssm_norm[0]), a_b, bmat, cmat, row(ssm_d[0]), wglu)
    hp, fin_p = _ssm_layer(x_prompt, h0_p, *ssm_args, tt=SSM_TIME_TILE)
    hs, fin_s = _ssm_layer(x_sample, h0_s, *ssm_args, tt=t_s)

    def split_state(fin, b_n):
        fin = fin.reshape(b_n, 2 * gp)
        return fin[:, :gp].reshape(1, b_n, g_n, p_n), fin[:, gp:].reshape(1, b_n, g_n, p_n)

    ssm_re_p, ssm_im_p = split_state(fin_p, b_p)
    ssm_re_s, ssm_im_s = split_state(fin_s, b_s)

    mlp0 = (row(mlp_norm[0]), wup[0], wdn[0])
    hp = _mlp(hp.reshape(n_p, d_model), *mlp0)
    hs = _mlp(hs.reshape(n_s, d_model), *mlp0)

    nb_p = t_p // MOBA_BLOCK
    cos_p, sin_p = _rope_cos_sin(jnp.arange(t_p, dtype=jnp.int32), head_dim)
    rope_pt = jnp.concatenate([cos_p.T, sin_p.T], axis=0)
    lanes_of = lambda g: jnp.broadcast_to(g[:, None], (head_dim, MOBA_BLOCK))
    k_p, v_p, kb_p, qt_p, vt_p, km_p = _kvq_prompt(
        hp, rope_pt, b_p, row(kv_norm), row(attn_norm[0]), wkv.T, wq.T,
        lanes_of(k_norm), lanes_of(q_norm[0] * (head_dim ** -0.5 * LOG2E)), head_dim)
    pos_s = past_len + jnp.arange(t_s, dtype=jnp.int32)
    rope_s = jnp.tile(_rope_table(pos_s, head_dim), (SAMPLE_KVQ_ROWS // t_s, 1))
    k_s, v_s, q_s = _kvq_sample(hs, rope_s, SAMPLE_KVQ_ROWS, row(kv_norm), row(attn_norm[0]),
                                wkv, wq, kn_t, qn_t, e_mat, head_dim)

    n_pairs = hd // LANES
    km_p = jnp.transpose(km_p.reshape(b_p, nb_p, n_pairs, LANES), (0, 2, 1, 3))
    km_p = jnp.pad(km_p, ((0, 0), (0, 0), (0, -nb_p % SUBLANES), (0, 0)))
    o_p = _attn_prompt(qt_p.reshape(b_p, n_pairs, LANES, t_p), kb_p,
                       vt_p.reshape(b_p, n_pairs, LANES, t_p), km_p, head_dim)
    cache_kt = jnp.transpose(cache_k, (0, 2, 3, 1)).reshape(n_pool, hd, page)
    cache_vt = jnp.transpose(cache_v, (0, 2, 3, 1)).reshape(n_pool, hd, page)
    o_s = _attn_sample(page_table, cache_kt, cache_vt, q_s.reshape(b_s, t_s, hd),
                       k_s.reshape(b_s, t_s, hd), v_s.reshape(b_s, t_s, hd), head_dim)

    mlp1 = (row(mlp_norm[1]), wup[1], wdn[1])
    y_p = _mlp(hp, *mlp1, o=o_p.reshape(n_p, hd), wo=wo)
    y_s = _mlp(hs, *mlp1, o=o_s.reshape(n_s, hd), wo=wo)

    to_bthd = lambda a: jnp.transpose(a.reshape(b_p, n_heads, head_dim, t_p), (0, 3, 1, 2))
    return (y_p.reshape(b_p, t_p, d_model), y_s.reshape(b_s, t_s, d_model),
            to_bthd(k_p), to_bthd(v_p),
            k_s.reshape(b_s, t_s, n_heads, head_dim), v_s.reshape(b_s, t_s, n_heads, head_dim),
            ssm_re_p, ssm_im_p, ssm_re_s, ssm_im_s)
```

```python
import functools

import jax
import jax.numpy as jnp
from jax import lax
from jax.experimental import pallas as pl
from jax.experimental.pallas import tpu as pltpu

F32 = jnp.float32
BF16 = jnp.bfloat16

EPS = 1e-6
NEG_INF = -1e30
ROPE_THETA = 500000.0
MOBA_BLOCK = 256
MOBA_TOPK = 3
LOG2E = 1.4426950408889634

SUBLANES = 8
LANES = 128
VMEM_LIMIT = 56 * 1024 * 1024

SSM_CHUNK_GROUPS = 8
SSM_TIME_TILE = 64
MLP_ROWS = 512
SAMPLE_KVQ_ROWS = 128
ATTN_PAIRS_PER_TRIP = 4


def _const_spec(shape):
    nd = len(shape)
    return pl.BlockSpec(shape, lambda *_: (0,) * nd, pipeline_mode=pl.Buffered(1))


def _params(sem):
    return pltpu.CompilerParams(dimension_semantics=sem, vmem_limit_bytes=VMEM_LIMIT)


def _rms(x):
    return x * lax.rsqrt(jnp.mean(x * x, axis=-1, keepdims=True) + EPS)


def _dot(a, b):
    return jnp.dot(a, b, preferred_element_type=F32)


def _dot_nt(a, b):
    return lax.dot_general(a, b, (((1,), (1,)), ((), ())), preferred_element_type=F32)


def _split_bf16(x):
    hi = x.astype(BF16)
    lo = (x - hi.astype(F32)).astype(BF16)
    return hi, lo


def _ssm_kernel(x_ref, h0_ref, g_ref, a_ref, bmat_ref, cmat_ref, d_ref, wglu_ref,
                out_ref, hfin_ref, s_ref, hst_ref, xs_ref, os_ref, *, tt, gp, nkc, kc, nc):
    ti = pl.program_id(1)
    d_model = x_ref.shape[-1]
    rows = tt * SUBLANES

    @pl.when(ti == 0)
    def _():
        hst_ref[...] = h0_ref[0]

    n_cb = d_model // LANES
    for b in range(SUBLANES):
        xb = x_ref[b]
        for c in range(n_cb):
            xs_ref[c, pl.ds(b, tt, stride=SUBLANES), :] = xb[:, c * LANES:(c + 1) * LANES]
    x = jnp.concatenate([xs_ref[c] for c in range(n_cb)], axis=1)
    u = _rms(x) * g_ref[...]
    ub = u.astype(BF16)

    def project_in(c):
        res = _dot(ub[:, c * kc:(c + 1) * kc], bmat_ref[c])
        s_ref[:, :, c * nc:(c + 1) * nc] = res[:, :nc].reshape(tt, SUBLANES, nc)
        s_ref[:, :, gp + c * nc:gp + (c + 1) * nc] = res[:, nc:].reshape(tt, SUBLANES, nc)

    def scan(c):
        re = slice(c * nc, (c + 1) * nc)
        im = slice(gp + c * nc, gp + (c + 1) * nc)
        ar, ai = a_ref[:, re], a_ref[:, im]
        hr, hi = hst_ref[:, re], hst_ref[:, im]
        for t in range(tt):
            hr, hi = (ar * hr - ai * hi + s_ref[t, :, re], ar * hi + ai * hr + s_ref[t, :, im])
            s_ref[t, :, re] = hr
            s_ref[t, :, im] = hi
        hst_ref[:, re] = hr
        hst_ref[:, im] = hi

    def project_out(c):
        sr = s_ref[:, :, c * nc:(c + 1) * nc].reshape(rows, nc).astype(BF16)
        si = s_ref[:, :, gp + c * nc:gp + (c + 1) * nc].reshape(rows, nc).astype(BF16)
        return _dot(sr, cmat_ref[c, :nc, :]) + _dot(si, cmat_ref[c, nc:, :])

    project_in(0)
    ys = []
    for c in range(nkc):
        if c + 1 < nkc:
            project_in(c + 1)
        scan(c)
        ys.append(project_out(c))
    y = jnp.concatenate(ys, axis=1) + d_ref[...] * u
    y = jax.nn.gelu(y, approximate=True)
    z = _dot(y.astype(BF16), wglu_ref[...])
    out = x + z[:, :d_model] * jax.nn.sigmoid(z[:, d_model:])
    for c in range(n_cb):
        os_ref[c] = out[:, c * LANES:(c + 1) * LANES]
    for b in range(SUBLANES):
        for c in range(n_cb):
            out_ref[b, :, c * LANES:(c + 1) * LANES] = os_ref[c, pl.ds(b, tt, stride=SUBLANES), :]

    @pl.when(ti == pl.num_programs(1) - 1)
    def _():
        hfin_ref[0] = hst_ref[...]


def _ssm_layer(x, h0, g, a_b, bmat, cmat, d, wglu, tt):
    b_tot, t_len, d_model = x.shape
    nbg = b_tot // SUBLANES
    gp2 = h0.shape[-1]
    gp = gp2 // 2
    nkc, kc, nc2 = bmat.shape
    kern = functools.partial(_ssm_kernel, tt=tt, gp=gp, nkc=nkc, kc=kc, nc=nc2 // 2)
    return pl.pallas_call(
        kern,
        grid=(nbg, t_len // tt),
        in_specs=[
            pl.BlockSpec((SUBLANES, tt, d_model), lambda b, t: (b, t, 0)),
            pl.BlockSpec((1, SUBLANES, gp2), lambda b, t: (b, 0, 0)),
            _const_spec(g.shape), _const_spec(a_b.shape), _const_spec(bmat.shape),
            _const_spec(cmat.shape), _const_spec(d.shape), _const_spec(wglu.shape),
        ],
        out_specs=[
            pl.BlockSpec((SUBLANES, tt, d_model), lambda b, t: (b, t, 0)),
            pl.BlockSpec((1, SUBLANES, gp2), lambda b, t: (b, 0, 0)),
        ],
        out_shape=[
            jax.ShapeDtypeStruct(x.shape, F32),
            jax.ShapeDtypeStruct(h0.shape, F32),
        ],
        scratch_shapes=[
            pltpu.VMEM((tt, SUBLANES, gp2), F32),
            pltpu.VMEM((SUBLANES, gp2), F32),
            pltpu.VMEM((d_model // LANES, tt * SUBLANES, LANES), F32),
            pltpu.VMEM((d_model // LANES, tt * SUBLANES, LANES), F32),
        ],
        compiler_params=_params(("arbitrary", "arbitrary")),
    )(x, h0, g, a_b, bmat, cmat, d, wglu)


def _ssm_weights(lam_re, lam_im, log_dt, b_re, b_im, c_re, c_im):
    g_n, p_n, c_n = b_re.shape
    dt = jnp.exp(log_dt)[:, None]
    mag = jnp.exp(lam_re * dt)
    ab_re, ab_im = mag * jnp.cos(lam_im * dt), mag * jnp.sin(lam_im * dt)
    nr, ni = ab_re - 1.0, ab_im
    den = lam_re * lam_re + lam_im * lam_im
    f_re = (nr * lam_re + ni * lam_im) / den
    f_im = (ni * lam_re - nr * lam_im) / den
    bb_re = f_re[..., None] * b_re - f_im[..., None] * b_im
    bb_im = f_re[..., None] * b_im + f_im[..., None] * b_re
    gk = SSM_CHUNK_GROUPS
    nkc = g_n // gk
    eye = jnp.eye(gk, dtype=F32)

    def bdiag_in(w):
        w = w.reshape(nkc, gk, p_n, c_n)
        return jnp.einsum('kgpc,gh->kgchp', w, eye).reshape(nkc, gk * c_n, gk * p_n)

    def bdiag_out(w):
        w = w.reshape(nkc, gk, c_n, p_n)
        return jnp.einsum('kgcp,gh->kgphc', w, eye).reshape(nkc, gk * p_n, gk * c_n)

    bmat = jnp.concatenate([bdiag_in(bb_re), bdiag_in(bb_im)], axis=2).astype(BF16)
    cmat = jnp.concatenate([bdiag_out(c_re), bdiag_out(-c_im)], axis=1).astype(BF16)
    a_row = jnp.concatenate([ab_re.reshape(-1), ab_im.reshape(-1)])
    a_b = jnp.broadcast_to(a_row[None, :], (SUBLANES, a_row.shape[0]))
    return a_b, bmat, cmat


def _mlp_kernel(*refs, pre):
    if pre:
        h_ref, o_ref, wo_ref, g_ref, wup_ref, wdn_ref, out_ref = refs
        h = h_ref[...] + _dot(o_ref[...].astype(BF16), wo_ref[...])
    else:
        h_ref, g_ref, wup_ref, wdn_ref, out_ref = refs
        h = h_ref[...]
    u = (_rms(h) * g_ref[...]).astype(BF16)
    a = jnp.square(jnp.maximum(_dot(u, wup_ref[...]), 0.0))
    out_ref[...] = h + _dot(a.astype(BF16), wdn_ref[...])


def _mlp(h, g, wup, wdn, o=None, wo=None):
    n_rows, d_model = h.shape
    blk = pl.BlockSpec((MLP_ROWS, d_model), lambda i: (i, 0))
    pre = o is not None
    in_specs = [blk]
    args = [h]
    if pre:
        in_specs += [pl.BlockSpec((MLP_ROWS, o.shape[-1]), lambda i: (i, 0)), _const_spec(wo.shape)]
        args += [o, wo]
    in_specs += [_const_spec(g.shape), _const_spec(wup.shape), _const_spec(wdn.shape)]
    args += [g, wup, wdn]
    return pl.pallas_call(
        functools.partial(_mlp_kernel, pre=pre),
        grid=(n_rows // MLP_ROWS,),
        in_specs=in_specs,
        out_specs=blk,
        out_shape=jax.ShapeDtypeStruct((n_rows, d_model), F32),
        compiler_params=_params(("arbitrary",)),
    )(*args)


def _kvq_kernel(h_ref, rope_ref, gkv_ref, gq_ref, wkv_ref, wq_ref, kn_ref, qn_ref, e_ref,
                k_ref, v_ref, q_ref, *, head_dim):
    hd = wq_ref.shape[-1]
    half = head_dim // 8
    hn = _rms(h_ref[...])
    kv = _dot((hn * gkv_ref[...]).astype(BF16), wkv_ref[...])
    q = _dot((hn * gq_ref[...]).astype(BF16), wq_ref[...])
    reps = hd // LANES
    rope = rope_ref[...]
    cos_t = jnp.tile(rope[:, :LANES], (1, reps))
    sin_lo = jnp.tile(rope[:, LANES:2 * LANES], (1, reps))
    sin_hi = jnp.tile(rope[:, 2 * LANES:], (1, reps))

    def headnorm_rope(x, gain):
        hi, lo = _split_bf16(x * x)
        ss = _dot(hi, e_ref[...]) + _dot(lo, e_ref[...])
        xn = x * lax.rsqrt(ss * (1.0 / head_dim) + EPS) * gain
        return (xn * cos_t + pltpu.roll(xn, hd - half, 1) * sin_lo
                + pltpu.roll(xn, half, 1) * sin_hi)

    k_ref[...] = headnorm_rope(kv[:, :hd], kn_ref[...])
    v_ref[...] = kv[:, hd:]
    q_ref[...] = headnorm_rope(q, qn_ref[...]) * (head_dim ** -0.5 * LOG2E)


def _kvq_sample(h, rope, rows, gkv, gq, wkv, wq, kn, qn, e_mat, head_dim):
    n_rows, d_model = h.shape
    hd = wq.shape[-1]
    row_map = lambda i: (i, 0)
    out = jax.ShapeDtypeStruct((n_rows, hd), F32)
    return pl.pallas_call(
        functools.partial(_kvq_kernel, head_dim=head_dim),
        grid=(n_rows // rows,),
        in_specs=[
            pl.BlockSpec((rows, d_model), row_map),
            _const_spec(rope.shape),
            _const_spec(gkv.shape), _const_spec(gq.shape), _const_spec(wkv.shape),
            _const_spec(wq.shape), _const_spec(kn.shape), _const_spec(qn.shape),
            _const_spec(e_mat.shape),
        ],
        out_specs=[pl.BlockSpec((rows, hd), row_map)] * 3,
        out_shape=[out, out, out],
        compiler_params=_params(("arbitrary",)),
    )(h, rope, gkv, gq, wkv, wq, kn, qn, e_mat)


def _kvq_prompt_kernel(h_ref, rope_ref, gkv_ref, gq_ref, wkv_ref, wq_ref, kn_ref, qn_ref,
                       k_ref, v_ref, kb_ref, qt_ref, vt_ref, km_ref, *, head_dim):
    hd = wq_ref.shape[0]
    rows = h_ref.shape[0]
    n_heads = hd // head_dim
    half = head_dim // 8
    hn = _rms(h_ref[...])
    kvt = _dot_nt(wkv_ref[...], (hn * gkv_ref[...]).astype(BF16))
    qt = _dot_nt(wq_ref[...], (hn * gq_ref[...]).astype(BF16))
    cos = rope_ref[0:half, :][None]
    sin = rope_ref[half:2 * half, :][None]

    def headnorm_rope_t(xt, gain):
        x3 = xt.reshape(n_heads, head_dim, rows)
        ss = jnp.sum(x3 * x3, axis=1, keepdims=True)
        xn = x3 * lax.rsqrt(ss * (1.0 / head_dim) + EPS) * gain[None]
        lo, hi = xn[:, 0:half, :], xn[:, half:2 * half, :]
        out = jnp.concatenate([lo * cos - hi * sin, hi * cos + lo * sin, xn[:, 2 * half:, :]], axis=1)
        return out.reshape(hd, rows)

    kt = headnorm_rope_t(kvt[:hd, :], kn_ref[...])
    vt = kvt[hd:, :]
    k_ref[0] = kt
    v_ref[0] = vt
    qt_ref[0] = headnorm_rope_t(qt, qn_ref[...]).astype(BF16)
    vt_ref[0] = vt.astype(BF16)
    k = kt.T
    km_ref[0] = jnp.mean(k, axis=0, keepdims=True)
    kb = k.astype(BF16)
    for p in range(hd // LANES):
        kb_ref[0, p] = kb[:, p * LANES:(p + 1) * LANES]


def _kvq_prompt(h, rope_t, n_seq, gkv, gq, wkv, wq, kn_b, qn_b, head_dim):
    n_rows, d_model = h.shape
    hd = wq.shape[0]
    rows = MOBA_BLOCK
    t_len = n_rows // n_seq
    nb = t_len // rows
    t_spec = pl.BlockSpec((1, hd, rows), lambda i: (i // nb, 0, i % nb))
    t_f32 = jax.ShapeDtypeStruct((n_seq, hd, t_len), F32)
    t_bf16 = jax.ShapeDtypeStruct((n_seq, hd, t_len), BF16)
    return pl.pallas_call(
        functools.partial(_kvq_prompt_kernel, head_dim=head_dim),
        grid=(n_seq * nb,),
        in_specs=[
            pl.BlockSpec((rows, d_model), lambda i: (i, 0)),
            pl.BlockSpec((rope_t.shape[0], rows), lambda i: (0, i % nb)),
            _const_spec(gkv.shape), _const_spec(gq.shape), _const_spec(wkv.shape),
            _const_spec(wq.shape), _const_spec(kn_b.shape), _const_spec(qn_b.shape),
        ],
        out_specs=[t_spec, t_spec,
                   pl.BlockSpec((1, hd // LANES, rows, LANES), lambda i: (i // nb, 0, i % nb, 0)),
                   t_spec, t_spec,
                   pl.BlockSpec((1, 1, hd), lambda i: (i, 0, 0))],
        out_shape=[t_f32, t_f32,
                   jax.ShapeDtypeStruct((n_seq, hd // LANES, t_len, LANES), BF16),
                   t_bf16, t_bf16,
                   jax.ShapeDtypeStruct((n_seq * nb, 1, hd), F32)],
        compiler_params=_params(("arbitrary",)),
    )(h, rope_t, gkv, gq, wkv, wq, kn_b, qn_b)


def _rope_cos_sin(pos, head_dim):
    half = head_dim // 8
    inv = ROPE_THETA ** (-jnp.arange(half, dtype=F32) / half)
    ang = pos.astype(F32)[:, None] * inv[None, :]
    return jnp.cos(ang), jnp.sin(ang)


def _rope_table(pos, head_dim):
    rot = head_dim // 4
    half = rot // 2
    cos, sin = _rope_cos_sin(pos, head_dim)
    j = jnp.arange(LANES) % head_dim
    cos_t = jnp.where(j[None, :] < rot, cos[:, j % half], 1.0)
    sin_lo = jnp.where(j[None, :] < half, -sin[:, j % half], 0.0)
    sin_hi = jnp.where((j[None, :] >= half) & (j[None, :] < rot), sin[:, j % half], 0.0)
    return jnp.concatenate([cos_t, sin_lo, sin_hi], axis=1)


def _attn_pairs(n_past, ps, qt_ref, kb_ref, vt_ref, km_ref, ot_ref, head_dim):
    blk = MOBA_BLOCK
    per = km_ref.shape[2]
    nk = (n_past + 1) * blk
    low = lax.broadcasted_iota(jnp.int32, (LANES, 1), 0) < head_dim
    causal = (lax.broadcasted_iota(jnp.int32, (blk, blk), 0)
              <= lax.broadcasted_iota(jnp.int32, (blk, blk), 1))
    gated = n_past > MOBA_TOPK
    blk_row = lax.broadcasted_iota(jnp.int32, (per, blk), 0)

    scores = []
    for p in ps:
        qt = qt_ref[0, p]
        zq = jnp.zeros_like(qt)
        k_all = kb_ref[0, p, 0:nk, :]
        for qh in (jnp.where(low, qt, zq), jnp.where(low, zq, qt)):
            s = _dot(k_all, qh)
            bias = None
            if gated:
                km_hi, km_lo = _split_bf16(km_ref[0, p])
                gate = _dot(km_hi, qh) + _dot(km_lo, qh)
                rank = jnp.zeros((per, blk), F32)
                for m in range(n_past):
                    gm = gate[m:m + 1, :]
                    rank = rank + jnp.where(blk_row > m, jnp.where(gm >= gate, 1.0, 0.0),
                                            jnp.where(gm > gate, 1.0, 0.0))
                bias = jnp.where(rank < MOBA_TOPK, 0.0, NEG_INF)
            scores.append((s, bias))

    for ip, p in enumerate(ps):
        v_all = jnp.concatenate([vt_ref[0, p, :, 0:nk], jnp.ones((2 * SUBLANES, nk), BF16)], axis=0)
        outs = []
        for hh in range(2):
            s, bias = scores[2 * ip + hh]
            pieces = [s[j * blk:(j + 1) * blk, :] for j in range(n_past)]
            if gated:
                pieces = [piece + bias[j:j + 1, :] for j, piece in enumerate(pieces)]
            pieces.append(jnp.where(causal, s[n_past * blk:, :], NEG_INF))
            m = jnp.max(pieces[0], axis=0, keepdims=True)
            for piece in pieces[1:]:
                m = jnp.maximum(m, jnp.max(piece, axis=0, keepdims=True))
            pes = [jnp.exp2(piece - m).astype(BF16) for piece in pieces]
            pe_all = jnp.concatenate(pes, axis=0) if n_past else pes[0]
            pv = _dot(v_all, pe_all)
            den = pv[LANES:LANES + 1, :]
            outs.append(pv[hh * head_dim:(hh + 1) * head_dim, :] / den)
        ot_ref[p] = jnp.concatenate(outs, axis=0)


def _attn_prompt_kernel(qt_ref, kb_ref, vt_ref, km_ref, o_ref, ot_ref, *, nb, head_dim):
    i = pl.program_id(1)
    n_pairs = ot_ref.shape[0]
    trips = n_pairs // ATTN_PAIRS_PER_TRIP
    for n in range(nb):
        @pl.when(i == n)
        def _(n=n):
            def some_pairs(p, carry):
                ps = tuple(p + r * trips for r in range(ATTN_PAIRS_PER_TRIP))
                _attn_pairs(n, ps, qt_ref, kb_ref, vt_ref, km_ref, ot_ref, head_dim)
                return carry
            lax.fori_loop(0, trips, some_pairs, 0)
    o_ref[0] = jnp.concatenate([ot_ref[p] for p in range(n_pairs)], axis=0).T.astype(o_ref.dtype)


def _attn_prompt(qt, kb, vt, kmean, head_dim):
    b_n, n_pairs, t_len, _ = kb.shape
    hd = n_pairs * LANES
    nb = t_len // MOBA_BLOCK
    per = kmean.shape[2]
    return pl.pallas_call(
        functools.partial(_attn_prompt_kernel, nb=nb, head_dim=head_dim),
        grid=(b_n, nb),
        in_specs=[
            pl.BlockSpec((1, n_pairs, LANES, MOBA_BLOCK), lambda b, i: (b, 0, 0, i)),
            pl.BlockSpec((1, n_pairs, t_len, LANES), lambda b, i: (b, 0, 0, 0)),
            pl.BlockSpec((1, n_pairs, LANES, t_len), lambda b, i: (b, 0, 0, 0)),
            pl.BlockSpec((1, n_pairs, per, LANES), lambda b, i: (b, 0, 0, 0)),
        ],
        out_specs=pl.BlockSpec((1, MOBA_BLOCK, hd), lambda b, i: (b, i, 0)),
        out_shape=jax.ShapeDtypeStruct((b_n, t_len, hd), BF16),
        scratch_shapes=[pltpu.VMEM((n_pairs, LANES, MOBA_BLOCK), F32)],
        compiler_params=_params(("arbitrary", "arbitrary")),
    )(qt, kb, vt, kmean)


def _attn_sample_kernel(pt_ref, *refs, npg, n_new, head_dim):
    del pt_ref
    ck = refs[:npg]
    cv = refs[npg:2 * npg]
    q_ref, kn_ref, vn_ref, o_ref, s_ref, p_ref, acc_ref, den_ref = refs[2 * npg:]
    page = ck[0].shape[2]
    hd = q_ref.shape[-1]
    ncol = s_ref.shape[1]
    ppb = MOBA_BLOCK // page
    n_past = npg // ppb

    @pl.when(pl.program_id(0) == 0)
    def _():
        p_ref[...] = jnp.zeros_like(p_ref)
        acc_ref[...] = jnp.zeros_like(acc_ref)
        den_ref[...] = jnp.ones_like(den_ref)

    r_head = lax.broadcasted_iota(jnp.int32, (ncol, hd), 0) // n_new
    c_head = lax.broadcasted_iota(jnp.int32, (ncol, hd), 1) // head_dim
    own_head = r_head == c_head
    wt = jnp.where(own_head, jnp.tile(q_ref[0], (ncol // n_new, 1)), 0.0).astype(BF16)

    for pg in range(npg):
        s_ref[pg] = _dot(wt, ck[pg][0].astype(BF16))

    acc = acc_ref[...]
    for pg in range(npg):
        acc = acc + _dot_nt(p_ref[pg], cv[pg][0].astype(BF16))
    diag = jnp.where(own_head, acc / den_ref[...], 0.0)
    out = diag[0:n_new, :]
    for h in range(1, ncol // n_new):
        out = out + diag[h * n_new:(h + 1) * n_new, :]
    o_ref[0] = out

    gates = []
    for n in range(n_past):
        tot = jnp.sum(s_ref[n * ppb], axis=1, keepdims=True)
        for r in range(1, ppb):
            tot = tot + jnp.sum(s_ref[n * ppb + r], axis=1, keepdims=True)
        gates.append(tot * (1.0 / MOBA_BLOCK))
    bias = []
    for n in range(n_past):
        rank = jnp.zeros((ncol, 1), F32)
        for m in range(n_past):
            if m != n:
                beats = (gates[m] >= gates[n]) if m < n else (gates[m] > gates[n])
                rank = rank + jnp.where(beats, 1.0, 0.0)
        bias.append(jnp.where(rank < MOBA_TOPK, 0.0, NEG_INF))

    kn = jnp.concatenate([kn_ref[0], jnp.zeros((page - n_new, hd), F32)], axis=0)
    s_own = _dot_nt(wt, kn.astype(BF16))
    qry_t = lax.broadcasted_iota(jnp.int32, (ncol, page), 0) % n_new
    key_t = lax.broadcasted_iota(jnp.int32, (ncol, page), 1)
    s_own = jnp.where(key_t <= qry_t, s_own, NEG_INF)

    mx = jnp.max(s_own, axis=1, keepdims=True)
    for pg in range(npg):
        mx = jnp.maximum(mx, jnp.max(s_ref[pg], axis=1, keepdims=True) + bias[pg // ppb])
    p_own = jnp.exp2(s_own - mx)
    den = jnp.sum(p_own, axis=1, keepdims=True)
    for pg in range(npg):
        pe = jnp.exp2(s_ref[pg] + (bias[pg // ppb] - mx))
        p_ref[pg] = pe.astype(BF16)
        den = den + jnp.sum(pe, axis=1, keepdims=True)
    den_ref[...] = den
    vn = jnp.concatenate([vn_ref[0], jnp.zeros((page - n_new, hd), F32)], axis=0)
    acc_ref[...] = _dot(p_own.astype(BF16), vn.astype(BF16))


def _attn_sample(page_table, cache_kt, cache_vt, q, k_new, v_new, head_dim):
    bs, n_new, hd = q.shape
    npg = page_table.shape[1]
    page = cache_kt.shape[2]
    ncol = (hd // head_dim) * n_new
    assert (npg * page) % MOBA_BLOCK == 0 and MOBA_BLOCK % page == 0 and n_new <= page
    assert ncol % SUBLANES == 0 and n_new % SUBLANES == 0

    cur = lambda b: jnp.minimum(b, bs - 1)
    prev = lambda b: jnp.maximum(b - 1, 0)

    def page_spec(r, seq):
        return pl.BlockSpec((1, hd, page), lambda b, pt: (pt[seq(b), r], 0, 0))

    seq_spec = pl.BlockSpec((1, n_new, hd), lambda b, pt: (cur(b), 0, 0))
    grid_spec = pltpu.PrefetchScalarGridSpec(
        num_scalar_prefetch=1,
        grid=(bs + 1,),
        in_specs=([page_spec(r, cur) for r in range(npg)] + [page_spec(r, prev) for r in range(npg)]
                  + [seq_spec, seq_spec, seq_spec]),
        out_specs=pl.BlockSpec((1, n_new, hd), lambda b, pt: (prev(b), 0, 0)),
        scratch_shapes=[
            pltpu.VMEM((npg, ncol, page), F32),
            pltpu.VMEM((npg, ncol, page), BF16),
            pltpu.VMEM((ncol, hd), F32),
            pltpu.VMEM((ncol, 1), F32),
        ],
    )
    return pl.pallas_call(
        functools.partial(_attn_sample_kernel, npg=npg, n_new=n_new, head_dim=head_dim),
        grid_spec=grid_spec,
        out_shape=jax.ShapeDtypeStruct((bs, n_new, hd), F32),
        compiler_params=_params(("arbitrary",)),
    )(page_table, *([cache_kt] * npg), *([cache_vt] * npg), q, k_new, v_new)


def kernel(x_prompt, x_sample, state_ssm_re, state_ssm_im, cache_k, cache_v, page_table,
           ssm_norm, ssm_lambda_re, ssm_lambda_im, ssm_log_dt, ssm_b_re, ssm_b_im,
           ssm_c_re, ssm_c_im, ssm_d, ssm_w_glu, kv_norm, w_kv, k_norm,
           attn_norm, w_q, q_norm, w_o, mlp_norm, w_up, w_down):
    b_p, t_p, d_model = x_prompt.shape
    b_s, t_s, _ = x_sample.shape
    n_pool, page, n_heads, head_dim = cache_k.shape
    hd = n_heads * head_dim
    g_n, p_n = ssm_lambda_re.shape[1:]
    gp = g_n * p_n
    past_len = page_table.shape[1] * page
    n_p, n_s = t_p * b_p, t_s * b_s
    assert ssm_norm.shape[0] == 1 and attn_norm.shape[0] == 1, "one S5 layer then one MoBA layer"
    assert b_p % SUBLANES == 0 and b_s % SUBLANES == 0 and t_p % MOBA_BLOCK == 0
    assert g_n % SSM_CHUNK_GROUPS == 0 and hd % LANES == 0
    assert t_p % SSM_TIME_TILE == 0 and n_p % MLP_ROWS == 0 and n_s % MLP_ROWS == 0
    assert SAMPLE_KVQ_ROWS % t_s == 0 and n_s % SAMPLE_KVQ_ROWS == 0

    row = lambda w: w.reshape(1, -1)
    a_b, bmat, cmat = _ssm_weights(ssm_lambda_re[0], ssm_lambda_im[0], ssm_log_dt[0],
                                   ssm_b_re[0], ssm_b_im[0], ssm_c_re[0], ssm_c_im[0])
    wglu = ssm_w_glu[0].astype(BF16)
    wup = [w_up[layer].astype(BF16) for layer in range(2)]
    wdn = [w_down[layer].astype(BF16) for layer in range(2)]
    wkv = w_kv.astype(BF16)
    wq = w_q[0].astype(BF16)
    wo = w_o[0].astype(BF16)
    head_of = jnp.arange(hd) // head_dim
    e_mat = (head_of[:, None] == head_of[None, :]).astype(BF16)
    kn_t = row(jnp.tile(k_norm, n_heads))
    qn_t = row(jnp.tile(q_norm[0], n_heads))

    h0_p = jnp.zeros((b_p // SUBLANES, SUBLANES, 2 * gp), F32)
    h0_s = jnp.concatenate([state_ssm_re[0].reshape(b_s, gp), state_ssm_im[0].reshape(b_s, gp)],
                           axis=1).reshape(b_s // SUBLANES, SUBLANES, 2 * gp)
    ssm_args = (row(ssm_norm[0]), a_b, bmat, cmat, row(ssm_d[0]), wglu)
    hp, fin_p = _ssm_layer(x_prompt, h0_p, *ssm_args, tt=SSM_TIME_TILE)
    hs, fin_s = _ssm_layer(x_sample, h0_s, *ssm_args, tt=t_s)

    def split_state(fin, b_n):
        fin = fin.reshape(b_n, 2 * gp)
        return fin[:, :gp].reshape(1, b_n, g_n, p_n), fin[:, gp:].reshape(1, b_n, g_n, p_n)

    ssm_re_p, ssm_im_p = split_state(fin_p, b_p)
    ssm_re_s, ssm_im_s = split_state(fin_s, b_s)

    mlp0 = (row(mlp_norm[0]), wup[0], wdn[0])
    hp = _mlp(hp.reshape(n_p, d_model), *mlp0)
    hs = _mlp(hs.reshape(n_s, d_model), *mlp0)

    nb_p = t_p // MOBA_BLOCK
    cos_p, sin_p = _rope_cos_sin(jnp.arange(t_p, dtype=jnp.int32), head_dim)
    rope_pt = jnp.concatenate([cos_p.T, sin_p.T], axis=0)
    lanes_of = lambda g: jnp.broadcast_to(g[:, None], (head_dim, MOBA_BLOCK))
    k_p, v_p, kb_p, qt_p, vt_p, km_p = _kvq_prompt(
        hp, rope_pt, b_p, row(kv_norm), row(attn_norm[0]), wkv.T, wq.T,
        lanes_of(k_norm), lanes_of(q_norm[0] * (head_dim ** -0.5 * LOG2E)), head_dim)
    pos_s = past_len + jnp.arange(t_s, dtype=jnp.int32)
    rope_s = jnp.tile(_rope_table(pos_s, head_dim), (SAMPLE_KVQ_ROWS // t_s, 1))
    k_s, v_s, q_s = _kvq_sample(hs, rope_s, SAMPLE_KVQ_ROWS, row(kv_norm), row(attn_norm[0]),
                                wkv, wq, kn_t, qn_t, e_mat, head_dim)

    n_pairs = hd // LANES
    km_p = jnp.transpose(km_p.reshape(b_p, nb_p, n_pairs, LANES), (0, 2, 1, 3))
    km_p = jnp.pad(km_p, ((0, 0), (0, 0), (0, -nb_p % SUBLANES), (0, 0)))
    o_p = _attn_prompt(qt_p.reshape(b_p, n_pairs, LANES, t_p), kb_p,
                       vt_p.reshape(b_p, n_pairs, LANES, t_p), km_p, head_dim)
    cache_kt = jnp.transpose(cache_k, (0, 2, 3, 1)).reshape(n_pool, hd, page)
    cache_vt = jnp.transpose(cache_v, (0, 2, 3, 1)).reshape(n_pool, hd, page)
    o_s = _attn_sample(page_table, cache_kt, cache_vt, q_s.reshape(b_s, t_s, hd),
                       k_s.reshape(b_s, t_s, hd), v_s.reshape(b_s, t_s, hd), head_dim)

    mlp1 = (row(mlp_norm[1]), wup[1], wdn[1])
    y_p = _mlp(hp, *mlp1, o=o_p.reshape(n_p, hd), wo=wo)
    y_s = _mlp(hs, *mlp1, o=o_s.reshape(n_s, hd), wo=wo)

    to_bthd = lambda a: jnp.transpose(a.reshape(b_p, n_heads, head_dim, t_p), (0, 3, 1, 2))
    return (y_p.reshape(b_p, t_p, d_model), y_s.reshape(b_s, t_s, d_model),
            to_bthd(k_p), to_bthd(v_p),
            k_s.reshape(b_s, t_s, n_heads, head_dim), v_s.reshape(b_s, t_s, n_heads, head_dim),
            ssm_re_p, ssm_im_p, ssm_re_s, ssm_im_s)
```

```python
import functools

import jax
import jax.numpy as jnp
from jax import lax
from jax.experimental import pallas as pl
from jax.experimental.pallas import tpu as pltpu

F32 = jnp.float32
BF16 = jnp.bfloat16

EPS = 1e-6
NEG_INF = -1e30
ROPE_THETA = 500000.0
MOBA_BLOCK = 256
MOBA_TOPK = 3
LOG2E = 1.4426950408889634

SUBLANES = 8
LANES = 128
VMEM_LIMIT = 56 * 1024 * 1024

SSM_CHUNK_GROUPS = 8
SSM_TIME_TILE = 64
MLP_ROWS = 512
SAMPLE_KVQ_ROWS = 512
ATTN_PAIRS_PER_TRIP = 4


def _const_spec(shape):
    nd = len(shape)
    return pl.BlockSpec(shape, lambda *_: (0,) * nd, pipeline_mode=pl.Buffered(1))


def _params(sem):
    return pltpu.CompilerParams(dimension_semantics=sem, vmem_limit_bytes=VMEM_LIMIT)


def _rms(x):
    return x * lax.rsqrt(jnp.mean(x * x, axis=-1, keepdims=True) + EPS)


def _dot(a, b):
    return jnp.dot(a, b, preferred_element_type=F32)


def _dot_nt(a, b):
    return lax.dot_general(a, b, (((1,), (1,)), ((), ())), preferred_element_type=F32)


def _split_bf16(x):
    hi = x.astype(BF16)
    lo = (x - hi.astype(F32)).astype(BF16)
    return hi, lo


def _ssm_kernel(x_ref, h0_ref, g_ref, a_ref, bmat_ref, cmat_ref, d_ref, wglu_ref,
                out_ref, hfin_ref, s_ref, hst_ref, xs_ref, os_ref, *, tt, gp, nkc, kc, nc):
    ti = pl.program_id(1)
    d_model = x_ref.shape[-1]
    rows = tt * SUBLANES

    @pl.when(ti == 0)
    def _():
        hst_ref[...] = h0_ref[0]

    n_cb = d_model // LANES
    for b in range(SUBLANES):
        xb = x_ref[b]
        for c in range(n_cb):
            xs_ref[c, pl.ds(b, tt, stride=SUBLANES), :] = xb[:, c * LANES:(c + 1) * LANES]
    x = jnp.concatenate([xs_ref[c] for c in range(n_cb)], axis=1)
    u = _rms(x) * g_ref[...]
    ub = u.astype(BF16)

    def project_in(c):
        res = _dot(ub[:, c * kc:(c + 1) * kc], bmat_ref[c])
        s_ref[:, :, c * nc:(c + 1) * nc] = res[:, :nc].reshape(tt, SUBLANES, nc)
        s_ref[:, :, gp + c * nc:gp + (c + 1) * nc] = res[:, nc:].reshape(tt, SUBLANES, nc)

    def scan(c):
        re = slice(c * nc, (c + 1) * nc)
        im = slice(gp + c * nc, gp + (c + 1) * nc)
        ar, ai = a_ref[:, re], a_ref[:, im]
        hr, hi = hst_ref[:, re], hst_ref[:, im]
        for t in range(tt):
            hr, hi = (ar * hr - ai * hi + s_ref[t, :, re], ar * hi + ai * hr + s_ref[t, :, im])
            s_ref[t, :, re] = hr
            s_ref[t, :, im] = hi
        hst_ref[:, re] = hr
        hst_ref[:, im] = hi

    def project_out(c):
        sr = s_ref[:, :, c * nc:(c + 1) * nc].reshape(rows, nc).astype(BF16)
        si = s_ref[:, :, gp + c * nc:gp + (c + 1) * nc].reshape(rows, nc).astype(BF16)
        return _dot(sr, cmat_ref[c, :nc, :]) + _dot(si, cmat_ref[c, nc:, :])

    project_in(0)
    ys = []
    for c in range(nkc):
        if c + 1 < nkc:
            project_in(c + 1)
        scan(c)
        ys.append(project_out(c))
    y = jnp.concatenate(ys, axis=1) + d_ref[...] * u
    y = jax.nn.gelu(y, approximate=True)
    z = _dot(y.astype(BF16), wglu_ref[...])
    out = x + z[:, :d_model] * jax.nn.sigmoid(z[:, d_model:])
    for c in range(n_cb):
        os_ref[c] = out[:, c * LANES:(c + 1) * LANES]
    for b in range(SUBLANES):
        for c in range(n_cb):
            out_ref[b, :, c * LANES:(c + 1) * LANES] = os_ref[c, pl.ds(b, tt, stride=SUBLANES), :]

    @pl.when(ti == pl.num_programs(1) - 1)
    def _():
        hfin_ref[0] = hst_ref[...]


def _ssm_layer(x, h0, g, a_b, bmat, cmat, d, wglu, tt):
    b_tot, t_len, d_model = x.shape
    nbg = b_tot // SUBLANES
    gp2 = h0.shape[-1]
    gp = gp2 // 2
    nkc, kc, nc2 = bmat.shape
    kern = functools.partial(_ssm_kernel, tt=tt, gp=gp, nkc=nkc, kc=kc, nc=nc2 // 2)
    return pl.pallas_call(
        kern,
        grid=(nbg, t_len // tt),
        in_specs=[
            pl.BlockSpec((SUBLANES, tt, d_model), lambda b, t: (b, t, 0)),
            pl.BlockSpec((1, SUBLANES, gp2), lambda b, t: (b, 0, 0)),
            _const_spec(g.shape), _const_spec(a_b.shape), _const_spec(bmat.shape),
            _const_spec(cmat.shape), _const_spec(d.shape), _const_spec(wglu.shape),
        ],
        out_specs=[
            pl.BlockSpec((SUBLANES, tt, d_model), lambda b, t: (b, t, 0)),
            pl.BlockSpec((1, SUBLANES, gp2), lambda b, t: (b, 0, 0)),
        ],
        out_shape=[
            jax.ShapeDtypeStruct(x.shape, F32),
            jax.ShapeDtypeStruct(h0.shape, F32),
        ],
        scratch_shapes=[
            pltpu.VMEM((tt, SUBLANES, gp2), F32),
            pltpu.VMEM((SUBLANES, gp2), F32),
            pltpu.VMEM((d_model // LANES, tt * SUBLANES, LANES), F32),
            pltpu.VMEM((d_model // LANES, tt * SUBLANES, LANES), F32),
        ],
        compiler_params=_params(("arbitrary", "arbitrary")),
    )(x, h0, g, a_b, bmat, cmat, d, wglu)


def _ssm_weights(lam_re, lam_im, log_dt, b_re, b_im, c_re, c_im):
    g_n, p_n, c_n = b_re.shape
    dt = jnp.exp(log_dt)[:, None]
    mag = jnp.exp(lam_re * dt)
    ab_re, ab_im = mag * jnp.cos(lam_im * dt), mag * jnp.sin(lam_im * dt)
    nr, ni = ab_re - 1.0, ab_im
    den = lam_re * lam_re + lam_im * lam_im
    f_re = (nr * lam_re + ni * lam_im) / den
    f_im = (ni * lam_re - nr * lam_im) / den
    bb_re = f_re[..., None] * b_re - f_im[..., None] * b_im
    bb_im = f_re[..., None] * b_im + f_im[..., None] * b_re
    gk = SSM_CHUNK_GROUPS
    nkc = g_n // gk
    eye = jnp.eye(gk, dtype=F32)

    def bdiag_in(w):
        w = w.reshape(nkc, gk, p_n, c_n)
        return jnp.einsum('kgpc,gh->kgchp', w, eye).reshape(nkc, gk * c_n, gk * p_n)

    def bdiag_out(w):
        w = w.reshape(nkc, gk, c_n, p_n)
        return jnp.einsum('kgcp,gh->kgphc', w, eye).reshape(nkc, gk * p_n, gk * c_n)

    bmat = jnp.concatenate([bdiag_in(bb_re), bdiag_in(bb_im)], axis=2).astype(BF16)
    cmat = jnp.concatenate([bdiag_out(c_re), bdiag_out(-c_im)], axis=1).astype(BF16)
    a_row = jnp.concatenate([ab_re.reshape(-1), ab_im.reshape(-1)])
    a_b = jnp.broadcast_to(a_row[None, :], (SUBLANES, a_row.shape[0]))
    return a_b, bmat, cmat


def _mlp_kernel(*refs, pre):
    if pre:
        h_ref, o_ref, wo_ref, g_ref, wup_ref, wdn_ref, out_ref = refs
        h = h_ref[...] + _dot(o_ref[...].astype(BF16), wo_ref[...])
    else:
        h_ref, g_ref, wup_ref, wdn_ref, out_ref = refs
        h = h_ref[...]
    u = (_rms(h) * g_ref[...]).astype(BF16)
    a = jnp.square(jnp.maximum(_dot(u, wup_ref[...]), 0.0))
    out_ref[...] = h + _dot(a.astype(BF16), wdn_ref[...])


def _mlp(h, g, wup, wdn, o=None, wo=None):
    n_rows, d_model = h.shape
    blk = pl.BlockSpec((MLP_ROWS, d_model), lambda i: (i, 0))
    pre = o is not None
    in_specs = [blk]
    args = [h]
    if pre:
        in_specs += [pl.BlockSpec((MLP_ROWS, o.shape[-1]), lambda i: (i, 0)), _const_spec(wo.shape)]
        args += [o, wo]
    in_specs += [_const_spec(g.shape), _const_spec(wup.shape), _const_spec(wdn.shape)]
    args += [g, wup, wdn]
    return pl.pallas_call(
        functools.partial(_mlp_kernel, pre=pre),
        grid=(n_rows // MLP_ROWS,),
        in_specs=in_specs,
        out_specs=blk,
        out_shape=jax.ShapeDtypeStruct((n_rows, d_model), F32),
        compiler_params=_params(("arbitrary",)),
    )(*args)


def _kvq_kernel(h_ref, rope_ref, gkv_ref, gq_ref, wkv_ref, wq_ref, kn_ref, qn_ref, e_ref,
                k_ref, v_ref, q_ref, *, head_dim):
    hd = wq_ref.shape[-1]
    half = head_dim // 8
    hn = _rms(h_ref[...])
    kv = _dot((hn * gkv_ref[...]).astype(BF16), wkv_ref[...])
    q = _dot((hn * gq_ref[...]).astype(BF16), wq_ref[...])
    reps = hd // LANES
    rope = rope_ref[...]
    cos_t = jnp.tile(rope[:, :LANES], (1, reps))
    sin_lo = jnp.tile(rope[:, LANES:2 * LANES], (1, reps))
    sin_hi = jnp.tile(rope[:, 2 * LANES:], (1, reps))

    def headnorm_rope(x, gain):
        hi, lo = _split_bf16(x * x)
        ss = _dot(hi, e_ref[...]) + _dot(lo, e_ref[...])
        xn = x * lax.rsqrt(ss * (1.0 / head_dim) + EPS) * gain
        return (xn * cos_t + pltpu.roll(xn, hd - half, 1) * sin_lo
                + pltpu.roll(xn, half, 1) * sin_hi)

    k_ref[...] = headnorm_rope(kv[:, :hd], kn_ref[...])
    v_ref[...] = kv[:, hd:]
    q_ref[...] = headnorm_rope(q, qn_ref[...]) * (head_dim ** -0.5 * LOG2E)


def _kvq_sample(h, rope, rows, gkv, gq, wkv, wq, kn, qn, e_mat, head_dim):
    n_rows, d_model = h.shape
    hd = wq.shape[-1]
    row_map = lambda i: (i, 0)
    out = jax.ShapeDtypeStruct((n_rows, hd), F32)
    return pl.pallas_call(
        functools.partial(_kvq_kernel, head_dim=head_dim),
        grid=(n_rows // rows,),
        in_specs=[
            pl.BlockSpec((rows, d_model), row_map),
            _const_spec(rope.shape),
            _const_spec(gkv.shape), _const_spec(gq.shape), _const_spec(wkv.shape),
            _const_spec(wq.shape), _const_spec(kn.shape), _const_spec(qn.shape),
            _const_spec(e_mat.shape),
        ],
        out_specs=[pl.BlockSpec((rows, hd), row_map)] * 3,
        out_shape=[out, out, out],
        compiler_params=_params(("arbitrary",)),
    )(h, rope, gkv, gq, wkv, wq, kn, qn, e_mat)


def _kvq_prompt_kernel(h_ref, rope_ref, gkv_ref, gq_ref, wkv_ref, wq_ref, kn_ref, qn_ref,
                       k_ref, v_ref, kb_ref, qt_ref, vt_ref, km_ref, *, head_dim):
    hd = wq_ref.shape[0]
    rows = h_ref.shape[0]
    n_heads = hd // head_dim
    half = head_dim // 8
    hn = _rms(h_ref[...])
    kvt = _dot_nt(wkv_ref[...], (hn * gkv_ref[...]).astype(BF16))
    qt = _dot_nt(wq_ref[...], (hn * gq_ref[...]).astype(BF16))
    cos = rope_ref[0:half, :][None]
    sin = rope_ref[half:2 * half, :][None]

    def headnorm_rope_t(xt, gain):
        x3 = xt.reshape(n_heads, head_dim, rows)
        ss = jnp.sum(x3 * x3, axis=1, keepdims=True)
        xn = x3 * lax.rsqrt(ss * (1.0 / head_dim) + EPS) * gain[None]
        lo, hi = xn[:, 0:half, :], xn[:, half:2 * half, :]
        out = jnp.concatenate([lo * cos - hi * sin, hi * cos + lo * sin, xn[:, 2 * half:, :]], axis=1)
        return out.reshape(hd, rows)

    kt = headnorm_rope_t(kvt[:hd, :], kn_ref[...])
    vt = kvt[hd:, :]
    k_ref[0] = kt
    v_ref[0] = vt
    qt_ref[0] = headnorm_rope_t(qt, qn_ref[...]).astype(BF16)
    vt_ref[0] = vt.astype(BF16)
    k = kt.T
    km_ref[0] = jnp.mean(k, axis=0, keepdims=True)
    kb = k.astype(BF16)
    for p in range(hd // LANES):
        kb_ref[0, p] = kb[:, p * LANES:(p + 1) * LANES]


def _kvq_prompt(h, rope_t, n_seq, gkv, gq, wkv, wq, kn_b, qn_b, head_dim):
    n_rows, d_model = h.shape
    hd = wq.shape[0]
    rows = MOBA_BLOCK
    t_len = n_rows // n_seq
    nb = t_len // rows
    t_spec = pl.BlockSpec((1, hd, rows), lambda i: (i // nb, 0, i % nb))
    t_f32 = jax.ShapeDtypeStruct((n_seq, hd, t_len), F32)
    t_bf16 = jax.ShapeDtypeStruct((n_seq, hd, t_len), BF16)
    return pl.pallas_call(
        functools.partial(_kvq_prompt_kernel, head_dim=head_dim),
        grid=(n_seq * nb,),
        in_specs=[
            pl.BlockSpec((rows, d_model), lambda i: (i, 0)),
            pl.BlockSpec((rope_t.shape[0], rows), lambda i: (0, i % nb)),
            _const_spec(gkv.shape), _const_spec(gq.shape), _const_spec(wkv.shape),
            _const_spec(wq.shape), _const_spec(kn_b.shape), _const_spec(qn_b.shape),
        ],
        out_specs=[t_spec, t_spec,
                   pl.BlockSpec((1, hd // LANES, rows, LANES), lambda i: (i // nb, 0, i % nb, 0)),
                   t_spec, t_spec,
                   pl.BlockSpec((1, 1, hd), lambda i: (i, 0, 0))],
        out_shape=[t_f32, t_f32,
                   jax.ShapeDtypeStruct((n_seq, hd // LANES, t_len, LANES), BF16),
                   t_bf16, t_bf16,
                   jax.ShapeDtypeStruct((n_seq * nb, 1, hd), F32)],
        compiler_params=_params(("arbitrary",)),
    )(h, rope_t, gkv, gq, wkv, wq, kn_b, qn_b)


def _rope_cos_sin(pos, head_dim):
    half = head_dim // 8
    inv = ROPE_THETA ** (-jnp.arange(half, dtype=F32) / half)
    ang = pos.astype(F32)[:, None] * inv[None, :]
    return jnp.cos(ang), jnp.sin(ang)


def _rope_table(pos, head_dim):
    rot = head_dim // 4
    half = rot // 2
    cos, sin = _rope_cos_sin(pos, head_dim)
    j = jnp.arange(LANES) % head_dim
    cos_t = jnp.where(j[None, :] < rot, cos[:, j % half], 1.0)
    sin_lo = jnp.where(j[None, :] < half, -sin[:, j % half], 0.0)
    sin_hi = jnp.where((j[None, :] >= half) & (j[None, :] < rot), sin[:, j % half], 0.0)
    return jnp.concatenate([cos_t, sin_lo, sin_hi], axis=1)


def _attn_pairs(n_past, ps, qt_ref, kb_ref, vt_ref, km_ref, ot_ref, head_dim):
    blk = MOBA_BLOCK
    per = km_ref.shape[2]
    nk = (n_past + 1) * blk
    low = lax.broadcasted_iota(jnp.int32, (LANES, 1), 0) < head_dim
    causal = (lax.broadcasted_iota(jnp.int32, (blk, blk), 0)
              <= lax.broadcasted_iota(jnp.int32, (blk, blk), 1))
    gated = n_past > MOBA_TOPK
    blk_row = lax.broadcasted_iota(jnp.int32, (per, blk), 0)

    scores = []
    for p in ps:
        qt = qt_ref[0, p]
        zq = jnp.zeros_like(qt)
        k_all = kb_ref[0, p, 0:nk, :]
        for qh in (jnp.where(low, qt, zq), jnp.where(low, zq, qt)):
            s = _dot(k_all, qh)
            bias = None
            if gated:
                km_hi, km_lo = _split_bf16(km_ref[0, p])
                gate = _dot(km_hi, qh) + _dot(km_lo, qh)
                rank = jnp.zeros((per, blk), F32)
                for m in range(n_past):
                    gm = gate[m:m + 1, :]
                    rank = rank + jnp.where(blk_row > m, jnp.where(gm >= gate, 1.0, 0.0),
                                            jnp.where(gm > gate, 1.0, 0.0))
                bias = jnp.where(rank < MOBA_TOPK, 0.0, NEG_INF)
            scores.append((s, bias))

    for ip, p in enumerate(ps):
        outs = []
        for hh in range(2):
            v_all = jnp.concatenate([vt_ref[0, p, hh * head_dim:(hh + 1) * head_dim, 0:nk],
                                     jnp.ones((2 * SUBLANES, nk), BF16)], axis=0)
            s, bias = scores[2 * ip + hh]
            pieces = [s[j * blk:(j + 1) * blk, :] for j in range(n_past)]
            if gated:
                pieces = [piece + bias[j:j + 1, :] for j, piece in enumerate(pieces)]
            pieces.append(jnp.where(causal, s[n_past * blk:, :], NEG_INF))
            m = jnp.max(pieces[0], axis=0, keepdims=True)
            for piece in pieces[1:]:
                m = jnp.maximum(m, jnp.max(piece, axis=0, keepdims=True))
            pes = [jnp.exp2(piece - m).astype(BF16) for piece in pieces]
            pe_all = jnp.concatenate(pes, axis=0) if n_past else pes[0]
            pv = _dot(v_all, pe_all)
            den = pv[head_dim:head_dim + 1, :]
            outs.append(pv[0:head_dim, :] / den)
        ot_ref[p] = jnp.concatenate(outs, axis=0)


def _attn_prompt_kernel(qt_ref, kb_ref, vt_ref, km_ref, o_ref, ot_ref, *, nb, head_dim):
    i = pl.program_id(1)
    n_pairs = ot_ref.shape[0]
    trips = n_pairs // ATTN_PAIRS_PER_TRIP
    for n in range(nb):
        @pl.when(i == n)
        def _(n=n):
            def some_pairs(p, carry):
                ps = tuple(p + r * trips for r in range(ATTN_PAIRS_PER_TRIP))
                _attn_pairs(n, ps, qt_ref, kb_ref, vt_ref, km_ref, ot_ref, head_dim)
                return carry
            lax.fori_loop(0, trips, some_pairs, 0)
    o_ref[0] = jnp.concatenate([ot_ref[p] for p in range(n_pairs)], axis=0).T.astype(o_ref.dtype)


def _attn_prompt(qt, kb, vt, kmean, head_dim):
    b_n, n_pairs, t_len, _ = kb.shape
    hd = n_pairs * LANES
    nb = t_len // MOBA_BLOCK
    per = kmean.shape[2]
    return pl.pallas_call(
        functools.partial(_attn_prompt_kernel, nb=nb, head_dim=head_dim),
        grid=(b_n, nb),
        in_specs=[
            pl.BlockSpec((1, n_pairs, LANES, MOBA_BLOCK), lambda b, i: (b, 0, 0, i)),
            pl.BlockSpec((1, n_pairs, t_len, LANES), lambda b, i: (b, 0, 0, 0)),
            pl.BlockSpec((1, n_pairs, LANES, t_len), lambda b, i: (b, 0, 0, 0)),
            pl.BlockSpec((1, n_pairs, per, LANES), lambda b, i: (b, 0, 0, 0)),
        ],
        out_specs=pl.BlockSpec((1, MOBA_BLOCK, hd), lambda b, i: (b, i, 0)),
        out_shape=jax.ShapeDtypeStruct((b_n, t_len, hd), BF16),
        scratch_shapes=[pltpu.VMEM((n_pairs, LANES, MOBA_BLOCK), F32)],
        compiler_params=_params(("arbitrary", "arbitrary")),
    )(qt, kb, vt, kmean)


def _attn_sample_kernel(pt_ref, *refs, npg, n_new, head_dim):
    del pt_ref
    ck = refs[:npg]
    cv = refs[npg:2 * npg]
    q_ref, kn_ref, vn_ref, o_ref, s_ref, p_ref, acc_ref, den_ref = refs[2 * npg:]
    page = ck[0].shape[2]
    hd = q_ref.shape[-1]
    ncol = s_ref.shape[1]
    ppb = MOBA_BLOCK // page
    n_past = npg // ppb

    @pl.when(pl.program_id(0) == 0)
    def _():
        p_ref[...] = jnp.zeros_like(p_ref)
        acc_ref[...] = jnp.zeros_like(acc_ref)
        den_ref[...] = jnp.ones_like(den_ref)

    r_head = lax.broadcasted_iota(jnp.int32, (ncol, hd), 0) // n_new
    c_head = lax.broadcasted_iota(jnp.int32, (ncol, hd), 1) // head_dim
    own_head = r_head == c_head
    wt = jnp.where(own_head, jnp.tile(q_ref[0], (ncol // n_new, 1)), 0.0).astype(BF16)

    for pg in range(npg):
        s_ref[pg] = _dot(wt, ck[pg][0].astype(BF16))

    acc = acc_ref[...]
    for pg in range(npg):
        acc = acc + _dot_nt(p_ref[pg], cv[pg][0].astype(BF16))
    diag = jnp.where(own_head, acc / den_ref[...], 0.0)
    out = diag[0:n_new, :]
    for h in range(1, ncol // n_new):
        out = out + diag[h * n_new:(h + 1) * n_new, :]
    o_ref[0] = out

    gates = []
    for n in range(n_past):
        tot = jnp.sum(s_ref[n * ppb], axis=1, keepdims=True)
        for r in range(1, ppb):
            tot = tot + jnp.sum(s_ref[n * ppb + r], axis=1, keepdims=True)
        gates.append(tot * (1.0 / MOBA_BLOCK))
    bias = []
    for n in range(n_past):
        rank = jnp.zeros((ncol, 1), F32)
        for m in range(n_past):
            if m != n:
                beats = (gates[m] >= gates[n]) if m < n else (gates[m] > gates[n])
                rank = rank + jnp.where(beats, 1.0, 0.0)
        bias.append(jnp.where(rank < MOBA_TOPK, 0.0, NEG_INF))

    kn = jnp.concatenate([kn_ref[0], jnp.zeros((page - n_new, hd), F32)], axis=0)
    s_own = _dot_nt(wt, kn.astype(BF16))
    qry_t = lax.broadcasted_iota(jnp.int32, (ncol, page), 0) % n_new
    key_t = lax.broadcasted_iota(jnp.int32, (ncol, page), 1)
    s_own = jnp.where(key_t <= qry_t, s_own, NEG_INF)

    mx = jnp.max(s_own, axis=1, keepdims=True)
    for pg in range(npg):
        mx = jnp.maximum(mx, jnp.max(s_ref[pg], axis=1, keepdims=True) + bias[pg // ppb])
    p_own = jnp.exp2(s_own - mx)
    den = jnp.sum(p_own, axis=1, keepdims=True)
    for pg in range(npg):
        pe = jnp.exp2(s_ref[pg] + (bias[pg // ppb] - mx))
        p_ref[pg] = pe.astype(BF16)
        den = den + jnp.sum(pe, axis=1, keepdims=True)
    den_ref[...] = den
    vn = jnp.concatenate([vn_ref[0], jnp.zeros((page - n_new, hd), F32)], axis=0)
    acc_ref[...] = _dot(p_own.astype(BF16), vn.astype(BF16))


def _attn_sample(page_table, cache_kt, cache_vt, q, k_new, v_new, head_dim):
    bs, n_new, hd = q.shape
    npg = page_table.shape[1]
    page = cache_kt.shape[2]
    ncol = (hd // head_dim) * n_new
    assert (npg * page) % MOBA_BLOCK == 0 and MOBA_BLOCK % page == 0 and n_new <= page
    assert ncol % SUBLANES == 0 and n_new % SUBLANES == 0

    cur = lambda b: jnp.minimum(b, bs - 1)
    prev = lambda b: jnp.maximum(b - 1, 0)

    def page_spec(r, seq):
        return pl.BlockSpec((1, hd, page), lambda b, pt: (pt[seq(b), r], 0, 0))

    seq_spec = pl.BlockSpec((1, n_new, hd), lambda b, pt: (cur(b), 0, 0))
    grid_spec = pltpu.PrefetchScalarGridSpec(
        num_scalar_prefetch=1,
        grid=(bs + 1,),
        in_specs=([page_spec(r, cur) for r in range(npg)] + [page_spec(r, prev) for r in range(npg)]
                  + [seq_spec, seq_spec, seq_spec]),
        out_specs=pl.BlockSpec((1, n_new, hd), lambda b, pt: (prev(b), 0, 0)),
        scratch_shapes=[
            pltpu.VMEM((npg, ncol, page), F32),
            pltpu.VMEM((npg, ncol, page), BF16),
            pltpu.VMEM((ncol, hd), F32),
            pltpu.VMEM((ncol, 1), F32),
        ],
    )
    return pl.pallas_call(
        functools.partial(_attn_sample_kernel, npg=npg, n_new=n_new, head_dim=head_dim),
        grid_spec=grid_spec,
        out_shape=jax.ShapeDtypeStruct((bs, n_new, hd), F32),
        compiler_params=_params(("arbitrary",)),
    )(page_table, *([cache_kt] * npg), *([cache_vt] * npg), q, k_new, v_new)


def kernel(x_prompt, x_sample, state_ssm_re, state_ssm_im, cache_k, cache_v, page_table,
           ssm_norm, ssm_lambda_re, ssm_lambda_im, ssm_log_dt, ssm_b_re, ssm_b_im,
           ssm_c_re, ssm_c_im, ssm_d, ssm_w_glu, kv_norm, w_kv, k_norm,
           attn_norm, w_q, q_norm, w_o, mlp_norm, w_up, w_down):
    b_p, t_p, d_model = x_prompt.shape
    b_s, t_s, _ = x_sample.shape
    n_pool, page, n_heads, head_dim = cache_k.shape
    hd = n_heads * head_dim
    g_n, p_n = ssm_lambda_re.shape[1:]
    gp = g_n * p_n
    past_len = page_table.shape[1] * page
    n_p, n_s = t_p * b_p, t_s * b_s
    assert ssm_norm.shape[0] == 1 and attn_norm.shape[0] == 1, "one S5 layer then one MoBA layer"
    assert b_p % SUBLANES == 0 and b_s % SUBLANES == 0 and t_p % MOBA_BLOCK == 0
    assert g_n % SSM_CHUNK_GROUPS == 0 and hd % LANES == 0
    assert t_p % SSM_TIME_TILE == 0 and n_p % MLP_ROWS == 0 and n_s % MLP_ROWS == 0
    assert SAMPLE_KVQ_ROWS % t_s == 0 and n_s % SAMPLE_KVQ_ROWS == 0

    row = lambda w: w.reshape(1, -1)
    a_b, bmat, cmat = _ssm_weights(ssm_lambda_re[0], ssm_lambda_im[0], ssm_log_dt[0],
                                   ssm_b_re[0], ssm_b_im[0], ssm_c_re[0], ssm_c_im[0])
    wglu = ssm_w_glu[0].astype(BF16)
    wup = [w_up[layer].astype(BF16) for layer in range(2)]
    wdn = [w_down[layer].astype(BF16) for layer in range(2)]
    wkv = w_kv.astype(BF16)
    wq = w_q[0].astype(BF16)
    wo = w_o[0].astype(BF16)
    head_of = jnp.arange(hd) // head_dim
    e_mat = (head_of[:, None] == head_of[None, :]).astype(BF16)
    kn_t = row(jnp.tile(k_norm, n_heads))
    qn_t = row(jnp.tile(q_norm[0], n_heads))

    h0_p = jnp.zeros((b_p // SUBLANES, SUBLANES, 2 * gp), F32)
    h0_s = jnp.concatenate([state_ssm_re[0].reshape(b_s, gp), state_ssm_im[0].reshape(b_s, gp)],
                           axis=1).reshape(b_s // SUBLANES, SUBLANES, 2 * gp)
    ssm_args = (row(ssm_norm[0]), a_b, bmat, cmat, row(ssm_d[0]), wglu)
    hp, fin_p = _ssm_layer(x_prompt, h0_p, *ssm_args, tt=SSM_TIME_TILE)
    hs, fin_s = _ssm_layer(x_sample, h0_s, *ssm_args, tt=t_s)

    def split_state(fin, b_n):
        fin = fin.reshape(b_n, 2 * gp)
        return fin[:, :gp].reshape(1, b_n, g_n, p_n), fin[:, gp:].reshape(1, b_n, g_n, p_n)

    ssm_re_p, ssm_im_p = split_state(fin_p, b_p)
    ssm_re_s, ssm_im_s = split_state(fin_s, b_s)

    mlp0 = (row(mlp_norm[0]), wup[0], wdn[0])
    hp = _mlp(hp.reshape(n_p, d_model), *mlp0)
    hs = _mlp(hs.reshape(n_s, d_model), *mlp0)

    nb_p = t_p // MOBA_BLOCK
    cos_p, sin_p = _rope_cos_sin(jnp.arange(t_p, dtype=jnp.int32), head_dim)
    rope_pt = jnp.concatenate([cos_p.T, sin_p.T], axis=0)
    lanes_of = lambda g: jnp.broadcast_to(g[:, None], (head_dim, MOBA_BLOCK))
    k_p, v_p, kb_p, qt_p, vt_p, km_p = _kvq_prompt(
        hp, rope_pt, b_p, row(kv_norm), row(attn_norm[0]), wkv.T, wq.T,
        lanes_of(k_norm), lanes_of(q_norm[0] * (head_dim ** -0.5 * LOG2E)), head_dim)
    pos_s = past_len + jnp.arange(t_s, dtype=jnp.int32)
    rope_s = jnp.tile(_rope_table(pos_s, head_dim), (SAMPLE_KVQ_ROWS // t_s, 1))
    k_s, v_s, q_s = _kvq_sample(hs, rope_s, SAMPLE_KVQ_ROWS, row(kv_norm), row(attn_norm[0]),
                                wkv, wq, kn_t, qn_t, e_mat, head_dim)

    n_pairs = hd // LANES
    km_p = jnp.transpose(km_p.reshape(b_p, nb_p, n_pairs, LANES), (0, 2, 1, 3))
    km_p = jnp.pad(km_p, ((0, 0), (0, 0), (0, -nb_p % SUBLANES), (0, 0)))
    o_p = _attn_prompt(qt_p.reshape(b_p, n_pairs, LANES, t_p), kb_p,
                       vt_p.reshape(b_p, n_pairs, LANES, t_p), km_p, head_dim)
    cache_kt = jnp.transpose(cache_k, (0, 2, 3, 1)).reshape(n_pool, hd, page)
    cache_vt = jnp.transpose(cache_v, (0, 2, 3, 1)).reshape(n_pool, hd, page)
    o_s = _attn_sample(page_table, cache_kt, cache_vt, q_s.reshape(b_s, t_s, hd),
                       k_s.reshape(b_s, t_s, hd), v_s.reshape(b_s, t_s, hd), head_dim)

    mlp1 = (row(mlp_norm[1]), wup[1], wdn[1])
    y_p = _mlp(hp, *mlp1, o=o_p.reshape(n_p, hd), wo=wo)
    y_s = _mlp(hs, *mlp1, o=o_s.reshape(n_s, hd), wo=wo)

    to_bthd = lambda a: jnp.transpose(a.reshape(b_p, n_heads, head_dim, t_p), (0, 3, 1, 2))
    return (y_p.reshape(b_p, t_p, d_model), y_s.reshape(b_s, t_s, d_model),
            to_bthd(k_p), to_bthd(v_p),
            k_s.reshape(b_s, t_s, n_heads, head_dim), v_s.reshape(b_s, t_s, n_heads, head_dim),
            ssm_re_p, ssm_im_p, ssm_re_s, ssm_im_s)
```

```python
import functools

import jax
import jax.numpy as jnp
from jax import lax
from jax.experimental import pallas as pl
from jax.experimental.pallas import tpu as pltpu

F32 = jnp.float32
BF16 = jnp.bfloat16

EPS = 1e-6
NEG_INF = -1e30
ROPE_THETA = 500000.0
MOBA_BLOCK = 256
MOBA_TOPK = 3
LOG2E = 1.4426950408889634

SUBLANES = 8
LANES = 128
VMEM_LIMIT = 56 * 1024 * 1024

SSM_CHUNK_GROUPS = 8
SSM_TIME_TILE = 64
MLP_ROWS = 512
SAMPLE_KVQ_ROWS = 128
ATTN_PAIRS_PER_TRIP = 4


def _const_spec(shape):
    nd = len(shape)
    return pl.BlockSpec(shape, lambda *_: (0,) * nd, pipeline_mode=pl.Buffered(1))


def _params(sem):
    return pltpu.CompilerParams(dimension_semantics=sem, vmem_limit_bytes=VMEM_LIMIT)


def _rms(x):
    return x * lax.rsqrt(jnp.mean(x * x, axis=-1, keepdims=True) + EPS)


def _dot(a, b):
    return jnp.dot(a, b, preferred_element_type=F32)


def _dot_nt(a, b):
    return lax.dot_general(a, b, (((1,), (1,)), ((), ())), preferred_element_type=F32)


def _split_bf16(x):
    hi = x.astype(BF16)
    lo = (x - hi.astype(F32)).astype(BF16)
    return hi, lo


def _ssm_kernel(x_ref, h0_ref, g_ref, a_ref, bmat_ref, cmat_ref, d_ref, wglu_ref,
                out_ref, hfin_ref, s_ref, hst_ref, xs_ref, os_ref, *, tt, gp, nkc, kc, nc):
    ti = pl.program_id(1)
    d_model = x_ref.shape[-1]
    rows = tt * SUBLANES

    @pl.when(ti == 0)
    def _():
        hst_ref[...] = h0_ref[0]

    n_cb = d_model // LANES
    for b in range(SUBLANES):
        xb = x_ref[b]
        for c in range(n_cb):
            xs_ref[c, pl.ds(b, tt, stride=SUBLANES), :] = xb[:, c * LANES:(c + 1) * LANES]
    x = jnp.concatenate([xs_ref[c] for c in range(n_cb)], axis=1)
    u = _rms(x) * g_ref[...]
    ub = u.astype(BF16)

    def project_in(c):
        res = _dot(ub[:, c * kc:(c + 1) * kc], bmat_ref[c])
        s_ref[:, :, c * nc:(c + 1) * nc] = res[:, :nc].reshape(tt, SUBLANES, nc)
        s_ref[:, :, gp + c * nc:gp + (c + 1) * nc] = res[:, nc:].reshape(tt, SUBLANES, nc)

    def scan(c):
        re = slice(c * nc, (c + 1) * nc)
        im = slice(gp + c * nc, gp + (c + 1) * nc)
        ar, ai = a_ref[:, re], a_ref[:, im]
        hr, hi = hst_ref[:, re], hst_ref[:, im]
        for t in range(tt):
            hr, hi = (ar * hr - ai * hi + s_ref[t, :, re], ar * hi + ai * hr + s_ref[t, :, im])
            s_ref[t, :, re] = hr
            s_ref[t, :, im] = hi
        hst_ref[:, re] = hr
        hst_ref[:, im] = hi

    def project_out(c):
        sr = s_ref[:, :, c * nc:(c + 1) * nc].reshape(rows, nc).astype(BF16)
        si = s_ref[:, :, gp + c * nc:gp + (c + 1) * nc].reshape(rows, nc).astype(BF16)
        return _dot(sr, cmat_ref[c, :nc, :]) + _dot(si, cmat_ref[c, nc:, :])

    project_in(0)
    ys = []
    for c in range(nkc):
        if c + 1 < nkc:
            project_in(c + 1)
        scan(c)
        ys.append(project_out(c))
    y = jnp.concatenate(ys, axis=1) + d_ref[...] * u
    y = jax.nn.gelu(y, approximate=True)
    z = _dot(y.astype(BF16), wglu_ref[...])
    out = x + z[:, :d_model] * jax.nn.sigmoid(z[:, d_model:])
    for c in range(n_cb):
        os_ref[c] = out[:, c * LANES:(c + 1) * LANES]
    for b in range(SUBLANES):
        for c in range(n_cb):
            out_ref[b, :, c * LANES:(c + 1) * LANES] = os_ref[c, pl.ds(b, tt, stride=SUBLANES), :]

    @pl.when(ti == pl.num_programs(1) - 1)
    def _():
        hfin_ref[0] = hst_ref[...]


def _ssm_layer(x, h0, g, a_b, bmat, cmat, d, wglu, tt):
    b_tot, t_len, d_model = x.shape
    nbg = b_tot // SUBLANES
    gp2 = h0.shape[-1]
    gp = gp2 // 2
    nkc, kc, nc2 = bmat.shape
    kern = functools.partial(_ssm_kernel, tt=tt, gp=gp, nkc=nkc, kc=kc, nc=nc2 // 2)
    return pl.pallas_call(
        kern,
        grid=(nbg, t_len // tt),
        in_specs=[
            pl.BlockSpec((SUBLANES, tt, d_model), lambda b, t: (b, t, 0)),
            pl.BlockSpec((1, SUBLANES, gp2), lambda b, t: (b, 0, 0)),
            _const_spec(g.shape), _const_spec(a_b.shape), _const_spec(bmat.shape),
            _const_spec(cmat.shape), _const_spec(d.shape), _const_spec(wglu.shape),
        ],
        out_specs=[
            pl.BlockSpec((SUBLANES, tt, d_model), lambda b, t: (b, t, 0)),
            pl.BlockSpec((1, SUBLANES, gp2), lambda b, t: (b, 0, 0)),
        ],
        out_shape=[
            jax.ShapeDtypeStruct(x.shape, F32),
            jax.ShapeDtypeStruct(h0.shape, F32),
        ],
        scratch_shapes=[
            pltpu.VMEM((tt, SUBLANES, gp2), F32),
            pltpu.VMEM((SUBLANES, gp2), F32),
            pltpu.VMEM((d_model // LANES, tt * SUBLANES, LANES), F32),
            pltpu.VMEM((d_model // LANES, tt * SUBLANES, LANES), F32),
        ],
        compiler_params=_params(("arbitrary", "arbitrary")),
    )(x, h0, g, a_b, bmat, cmat, d, wglu)


def _ssm_weights(lam_re, lam_im, log_dt, b_re, b_im, c_re, c_im):
    g_n, p_n, c_n = b_re.shape
    dt = jnp.exp(log_dt)[:, None]
    mag = jnp.exp(lam_re * dt)
    ab_re, ab_im = mag * jnp.cos(lam_im * dt), mag * jnp.sin(lam_im * dt)
    nr, ni = ab_re - 1.0, ab_im
    den = lam_re * lam_re + lam_im * lam_im
    f_re = (nr * lam_re + ni * lam_im) / den
    f_im = (ni * lam_re - nr * lam_im) / den
    bb_re = f_re[..., None] * b_re - f_im[..., None] * b_im
    bb_im = f_re[..., None] * b_im + f_im[..., None] * b_re
    gk = SSM_CHUNK_GROUPS
    nkc = g_n // gk
    eye = jnp.eye(gk, dtype=F32)

    def bdiag_in(w):
        w = w.reshape(nkc, gk, p_n, c_n)
        return jnp.einsum('kgpc,gh->kgchp', w, eye).reshape(nkc, gk * c_n, gk * p_n)

    def bdiag_out(w):
        w = w.reshape(nkc, gk, c_n, p_n)
        return jnp.einsum('kgcp,gh->kgphc', w, eye).reshape(nkc, gk * p_n, gk * c_n)

    bmat = jnp.concatenate([bdiag_in(bb_re), bdiag_in(bb_im)], axis=2).astype(BF16)
    cmat = jnp.concatenate([bdiag_out(c_re), bdiag_out(-c_im)], axis=1).astype(BF16)
    a_row = jnp.concatenate([ab_re.reshape(-1), ab_im.reshape(-1)])
    a_b = jnp.broadcast_to(a_row[None, :], (SUBLANES, a_row.shape[0]))
    return a_b, bmat, cmat


def _mlp_kernel(*refs, pre):
    if pre:
        h_ref, o_ref, wo_ref, g_ref, wup_ref, wdn_ref, out_ref = refs
        h = h_ref[...] + _dot(o_ref[...].astype(BF16), wo_ref[...])
    else:
        h_ref, g_ref, wup_ref, wdn_ref, out_ref = refs
        h = h_ref[...]
    u = (_rms(h) * g_ref[...]).astype(BF16)
    a = jnp.square(jnp.maximum(_dot(u, wup_ref[...]), 0.0))
    out_ref[...] = h + _dot(a.astype(BF16), wdn_ref[...])


def _mlp(h, g, wup, wdn, o=None, wo=None):
    n_rows, d_model = h.shape
    blk = pl.BlockSpec((MLP_ROWS, d_model), lambda i: (i, 0))
    pre = o is not None
    in_specs = [blk]
    args = [h]
    if pre:
        in_specs += [pl.BlockSpec((MLP_ROWS, o.shape[-1]), lambda i: (i, 0)), _const_spec(wo.shape)]
        args += [o, wo]
    in_specs += [_const_spec(g.shape), _const_spec(wup.shape), _const_spec(wdn.shape)]
    args += [g, wup, wdn]
    return pl.pallas_call(
        functools.partial(_mlp_kernel, pre=pre),
        grid=(n_rows // MLP_ROWS,),
        in_specs=in_specs,
        out_specs=blk,
        out_shape=jax.ShapeDtypeStruct((n_rows, d_model), F32),
        compiler_params=_params(("arbitrary",)),
    )(*args)


def _kvq_kernel(h_ref, rope_ref, gkv_ref, gq_ref, wkv_ref, wq_ref, kn_ref, qn_ref, e_ref,
                k_ref, v_ref, q_ref, *, head_dim):
    hd = wq_ref.shape[-1]
    half = head_dim // 8
    hn = _rms(h_ref[...])
    kv = _dot((hn * gkv_ref[...]).astype(BF16), wkv_ref[...])
    q = _dot((hn * gq_ref[...]).astype(BF16), wq_ref[...])
    reps = hd // LANES
    rope = rope_ref[...]
    cos_t = jnp.tile(rope[:, :LANES], (1, reps))
    sin_lo = jnp.tile(rope[:, LANES:2 * LANES], (1, reps))
    sin_hi = jnp.tile(rope[:, 2 * LANES:], (1, reps))

    def headnorm_rope(x, gain):
        hi, lo = _split_bf16(x * x)
        ss = _dot(hi, e_ref[...]) + _dot(lo, e_ref[...])
        xn = x * lax.rsqrt(ss * (1.0 / head_dim) + EPS) * gain
        return (xn * cos_t + pltpu.roll(xn, hd - half, 1) * sin_lo
                + pltpu.roll(xn, half, 1) * sin_hi)

    k_ref[...] = headnorm_rope(kv[:, :hd], kn_ref[...])
    v_ref[...] = kv[:, hd:]
    q_ref[...] = headnorm_rope(q, qn_ref[...]) * (head_dim ** -0.5 * LOG2E)


def _kvq_sample(h, rope, rows, gkv, gq, wkv, wq, kn, qn, e_mat, head_dim):
    n_rows, d_model = h.shape
    hd = wq.shape[-1]
    row_map = lambda i: (i, 0)
    out = jax.ShapeDtypeStruct((n_rows, hd), F32)
    return pl.pallas_call(
        functools.partial(_kvq_kernel, head_dim=head_dim),
        grid=(n_rows // rows,),
        in_specs=[
            pl.BlockSpec((rows, d_model), row_map),
            _const_spec(rope.shape),
            _const_spec(gkv.shape), _const_spec(gq.shape), _const_spec(wkv.shape),
            _const_spec(wq.shape), _const_spec(kn.shape), _const_spec(qn.shape),
            _const_spec(e_mat.shape),
        ],
        out_specs=[pl.BlockSpec((rows, hd), row_map)] * 3,
        out_shape=[out, out, out],
        compiler_params=_params(("arbitrary",)),
    )(h, rope, gkv, gq, wkv, wq, kn, qn, e_mat)


def _kvq_prompt_kernel(h_ref, rope_ref, gkv_ref, gq_ref, wkv_ref, wq_ref, kn_ref, qn_ref,
                       k_ref, v_ref, kb_ref, qt_ref, vt_ref, km_ref, *, head_dim):
    hd = wq_ref.shape[0]
    rows = h_ref.shape[0]
    n_heads = hd // head_dim
    half = head_dim // 8
    hn = _rms(h_ref[...])
    kvt = _dot_nt(wkv_ref[...], (hn * gkv_ref[...]).astype(BF16))
    qt = _dot_nt(wq_ref[...], (hn * gq_ref[...]).astype(BF16))
    cos = rope_ref[0:half, :][None]
    sin = rope_ref[half:2 * half, :][None]

    def headnorm_rope_t(xt, gain):
        x3 = xt.reshape(n_heads, head_dim, rows)
        ss = jnp.sum(x3 * x3, axis=1, keepdims=True)
        xn = x3 * lax.rsqrt(ss * (1.0 / head_dim) + EPS) * gain[None]
        lo, hi = xn[:, 0:half, :], xn[:, half:2 * half, :]
        out = jnp.concatenate([lo * cos - hi * sin, hi * cos + lo * sin, xn[:, 2 * half:, :]], axis=1)
        return out.reshape(hd, rows)

    kt = headnorm_rope_t(kvt[:hd, :], kn_ref[...])
    vt = kvt[hd:, :]
    k_ref[0] = kt
    v_ref[0] = vt
    qt_ref[0] = headnorm_rope_t(qt, qn_ref[...]).astype(BF16)
    vt_ref[0] = vt.astype(BF16)
    k = kt.T
    km_ref[0] = jnp.mean(k, axis=0, keepdims=True)
    kb = k.astype(BF16)
    for p in range(hd // LANES):
        kb_ref[0, p] = kb[:, p * LANES:(p + 1) * LANES]


def _kvq_prompt(h, rope_t, n_seq, gkv, gq, wkv, wq, kn_b, qn_b, head_dim):
    n_rows, d_model = h.shape
    hd = wq.shape[0]
    rows = MOBA_BLOCK
    t_len = n_rows // n_seq
    nb = t_len // rows
    t_spec = pl.BlockSpec((1, hd, rows), lambda i: (i // nb, 0, i % nb))
    t_f32 = jax.ShapeDtypeStruct((n_seq, hd, t_len), F32)
    t_bf16 = jax.ShapeDtypeStruct((n_seq, hd, t_len), BF16)
    return pl.pallas_call(
        functools.partial(_kvq_prompt_kernel, head_dim=head_dim),
        grid=(n_seq * nb,),
        in_specs=[
            pl.BlockSpec((rows, d_model), lambda i: (i, 0)),
            pl.BlockSpec((rope_t.shape[0], rows), lambda i: (0, i % nb)),
            _const_spec(gkv.shape), _const_spec(gq.shape), _const_spec(wkv.shape),
            _const_spec(wq.shape), _const_spec(kn_b.shape), _const_spec(qn_b.shape),
        ],
        out_specs=[t_spec, t_spec,
                   pl.BlockSpec((1, hd // LANES, rows, LANES), lambda i: (i // nb, 0, i % nb, 0)),
                   t_spec, t_spec,
                   pl.BlockSpec((1, 1, hd), lambda i: (i, 0, 0))],
        out_shape=[t_f32, t_f32,
                   jax.ShapeDtypeStruct((n_seq, hd // LANES, t_len, LANES), BF16),
                   t_bf16, t_bf16,
                   jax.ShapeDtypeStruct((n_seq * nb, 1, hd), F32)],
        compiler_params=_params(("arbitrary",)),
    )(h, rope_t, gkv, gq, wkv, wq, kn_b, qn_b)


def _rope_cos_sin(pos, head_dim):
    half = head_dim // 8
    inv = ROPE_THETA ** (-jnp.arange(half, dtype=F32) / half)
    ang = pos.astype(F32)[:, None] * inv[None, :]
    return jnp.cos(ang), jnp.sin(ang)


def _rope_table(pos, head_dim):
    rot = head_dim // 4
    half = rot // 2
    cos, sin = _rope_cos_sin(pos, head_dim)
    j = jnp.arange(LANES) % head_dim
    cos_t = jnp.where(j[None, :] < rot, cos[:, j % half], 1.0)
    sin_lo = jnp.where(j[None, :] < half, -sin[:, j % half], 0.0)
    sin_hi = jnp.where((j[None, :] >= half) & (j[None, :] < rot), sin[:, j % half], 0.0)
    return jnp.concatenate([cos_t, sin_lo, sin_hi], axis=1)


def _attn_pairs(n_past, ps, qt_ref, kb_ref, vt_ref, km_ref, ot_ref, head_dim):
    blk = MOBA_BLOCK
    per = km_ref.shape[2]
    nk = (n_past + 1) * blk
    low = lax.broadcasted_iota(jnp.int32, (LANES, 1), 0) < head_dim
    causal = (lax.broadcasted_iota(jnp.int32, (blk, blk), 0)
              <= lax.broadcasted_iota(jnp.int32, (blk, blk), 1))
    gated = n_past > MOBA_TOPK
    blk_row = lax.broadcasted_iota(jnp.int32, (per, blk), 0)

    scores = []
    for p in ps:
        qt = qt_ref[0, p]
        zq = jnp.zeros_like(qt)
        k_all = kb_ref[0, p, 0:nk, :]
        for qh in (jnp.where(low, qt, zq), jnp.where(low, zq, qt)):
            s = _dot(k_all, qh)
            bias = None
            if gated:
                km_hi, km_lo = _split_bf16(km_ref[0, p])
                gate = _dot(km_hi, qh) + _dot(km_lo, qh)
                rank = jnp.zeros((per, blk), F32)
                for m in range(n_past):
                    gm = gate[m:m + 1, :]
                    rank = rank + jnp.where(blk_row > m, jnp.where(gm >= gate, 1.0, 0.0),
                                            jnp.where(gm > gate, 1.0, 0.0))
                bias = jnp.where(rank < MOBA_TOPK, 0.0, NEG_INF)
            scores.append((s, bias))

    for ip, p in enumerate(ps):
        outs = []
        for hh in range(2):
            v_all = vt_ref[0, p, hh * head_dim:(hh + 1) * head_dim, 0:nk]
            s, bias = scores[2 * ip + hh]
            pieces = [s[j * blk:(j + 1) * blk, :] for j in range(n_past)]
            if gated:
                pieces = [piece + bias[j:j + 1, :] for j, piece in enumerate(pieces)]
            pieces.append(jnp.where(causal, s[n_past * blk:, :], NEG_INF))
            m = jnp.max(pieces[0], axis=0, keepdims=True)
            for piece in pieces[1:]:
                m = jnp.maximum(m, jnp.max(piece, axis=0, keepdims=True))
            pes = [jnp.exp2(piece - m) for piece in pieces]
            den = jnp.sum(pes[0], axis=0, keepdims=True)
            for pe in pes[1:]:
                den = den + jnp.sum(pe, axis=0, keepdims=True)
            pes = [pe.astype(BF16) for pe in pes]
            pe_all = jnp.concatenate(pes, axis=0) if n_past else pes[0]
            outs.append(_dot(v_all, pe_all) / den)
        ot_ref[p] = jnp.concatenate(outs, axis=0)


def _attn_prompt_kernel(qt_ref, kb_ref, vt_ref, km_ref, o_ref, ot_ref, *, nb, head_dim):
    i = pl.program_id(1)
    n_pairs = ot_ref.shape[0]
    trips = n_pairs // ATTN_PAIRS_PER_TRIP
    for n in range(nb):
        @pl.when(i == n)
        def _(n=n):
            def some_pairs(p, carry):
                ps = tuple(p + r * trips for r in range(ATTN_PAIRS_PER_TRIP))
                _attn_pairs(n, ps, qt_ref, kb_ref, vt_ref, km_ref, ot_ref, head_dim)
                return carry
            lax.fori_loop(0, trips, some_pairs, 0)
    o_ref[0] = jnp.concatenate([ot_ref[p] for p in range(n_pairs)], axis=0).T.astype(o_ref.dtype)


def _attn_prompt(qt, kb, vt, kmean, head_dim):
    b_n, n_pairs, t_len, _ = kb.shape
    hd = n_pairs * LANES
    nb = t_len // MOBA_BLOCK
    per = kmean.shape[2]
    return pl.pallas_call(
        functools.partial(_attn_prompt_kernel, nb=nb, head_dim=head_dim),
        grid=(b_n, nb),
        in_specs=[
            pl.BlockSpec((1, n_pairs, LANES, MOBA_BLOCK), lambda b, i: (b, 0, 0, i)),
            pl.BlockSpec((1, n_pairs, t_len, LANES), lambda b, i: (b, 0, 0, 0)),
            pl.BlockSpec((1, n_pairs, LANES, t_len), lambda b, i: (b, 0, 0, 0)),
            pl.BlockSpec((1, n_pairs, per, LANES), lambda b, i: (b, 0, 0, 0)),
        ],
        out_specs=pl.BlockSpec((1, MOBA_BLOCK, hd), lambda b, i: (b, i, 0)),
        out_shape=jax.ShapeDtypeStruct((b_n, t_len, hd), BF16),
        scratch_shapes=[pltpu.VMEM((n_pairs, LANES, MOBA_BLOCK), F32)],
        compiler_params=_params(("arbitrary", "arbitrary")),
    )(qt, kb, vt, kmean)


def _attn_sample_kernel(pt_ref, *refs, npg, n_new, head_dim):
    del pt_ref
    ck = refs[:npg]
    cv = refs[npg:2 * npg]
    q_ref, kn_ref, vn_ref, o_ref, s_ref, p_ref, acc_ref, den_ref = refs[2 * npg:]
    page = ck[0].shape[2]
    hd = q_ref.shape[-1]
    ncol = s_ref.shape[1]
    ppb = MOBA_BLOCK // page
    n_past = npg // ppb

    @pl.when(pl.program_id(0) == 0)
    def _():
        p_ref[...] = jnp.zeros_like(p_ref)
        acc_ref[...] = jnp.zeros_like(acc_ref)
        den_ref[...] = jnp.ones_like(den_ref)

    r_head = lax.broadcasted_iota(jnp.int32, (ncol, hd), 0) // n_new
    c_head = lax.broadcasted_iota(jnp.int32, (ncol, hd), 1) // head_dim
    own_head = r_head == c_head
    wt = jnp.where(own_head, jnp.tile(q_ref[0], (ncol // n_new, 1)), 0.0).astype(BF16)

    for pg in range(npg):
        s_ref[pg] = _dot(wt, ck[pg][0].astype(BF16))

    acc = acc_ref[...]
    for pg in range(npg):
        acc = acc + _dot_nt(p_ref[pg], cv[pg][0].astype(BF16))
    diag = jnp.where(own_head, acc / den_ref[...], 0.0)
    out = diag[0:n_new, :]
    for h in range(1, ncol // n_new):
        out = out + diag[h * n_new:(h + 1) * n_new, :]
    o_ref[0] = out

    gates = []
    for n in range(n_past):
        tot = jnp.sum(s_ref[n * ppb], axis=1, keepdims=True)
        for r in range(1, ppb):
            tot = tot + jnp.sum(s_ref[n * ppb + r], axis=1, keepdims=True)
        gates.append(tot * (1.0 / MOBA_BLOCK))
    bias = []
    for n in range(n_past):
        rank = jnp.zeros((ncol, 1), F32)
        for m in range(n_past):
            if m != n:
                beats = (gates[m] >= gates[n]) if m < n else (gates[m] > gates[n])
                rank = rank + jnp.where(beats, 1.0, 0.0)
        bias.append(jnp.where(rank < MOBA_TOPK, 0.0, NEG_INF))

    kn = jnp.concatenate([kn_ref[0], jnp.zeros((page - n_new, hd), F32)], axis=0)
    s_own = _dot_nt(wt, kn.astype(BF16))
    qry_t = lax.broadcasted_iota(jnp.int32, (ncol, page), 0) % n_new
    key_t = lax.broadcasted_iota(jnp.int32, (ncol, page), 1)
    s_own = jnp.where(key_t <= qry_t, s_own, NEG_INF)

    mx = jnp.max(s_own, axis=1, keepdims=True)
    for pg in range(npg):
        mx = jnp.maximum(mx, jnp.max(s_ref[pg], axis=1, keepdims=True) + bias[pg // ppb])
    p_own = jnp.exp2(s_own - mx)
    den = jnp.sum(p_own, axis=1, keepdims=True)
    for pg in range(npg):
        pe = jnp.exp2(s_ref[pg] + (bias[pg // ppb] - mx))
        p_ref[pg] = pe.astype(BF16)
        den = den + jnp.sum(pe, axis=1, keepdims=True)
    den_ref[...] = den
    vn = jnp.concatenate([vn_ref[0], jnp.zeros((page - n_new, hd), F32)], axis=0)
    acc_ref[...] = _dot(p_own.astype(BF16), vn.astype(BF16))


def _attn_sample(page_table, cache_kt, cache_vt, q, k_new, v_new, head_dim):
    bs, n_new, hd = q.shape
    npg = page_table.shape[1]
    page = cache_kt.shape[2]
    ncol = (hd // head_dim) * n_new
    assert (npg * page) % MOBA_BLOCK == 0 and MOBA_BLOCK % page == 0 and n_new <= page
    assert ncol % SUBLANES == 0 and n_new % SUBLANES == 0

    cur = lambda b: jnp.minimum(b, bs - 1)
    prev = lambda b: jnp.maximum(b - 1, 0)

    def page_spec(r, seq):
        return pl.BlockSpec((1, hd, page), lambda b, pt: (pt[seq(b), r], 0, 0))

    seq_spec = pl.BlockSpec((1, n_new, hd), lambda b, pt: (cur(b), 0, 0))
    grid_spec = pltpu.PrefetchScalarGridSpec(
        num_scalar_prefetch=1,
        grid=(bs + 1,),
        in_specs=([page_spec(r, cur) for r in range(npg)] + [page_spec(r, prev) for r in range(npg)]
                  + [seq_spec, seq_spec, seq_spec]),
        out_specs=pl.BlockSpec((1, n_new, hd), lambda b, pt: (prev(b), 0, 0)),
        scratch_shapes=[
            pltpu.VMEM((npg, ncol, page), F32),
            pltpu.VMEM((npg, ncol, page), BF16),
            pltpu.VMEM((ncol, hd), F32),
            pltpu.VMEM((ncol, 1), F32),
        ],
    )
    return pl.pallas_call(
        functools.partial(_attn_sample_kernel, npg=npg, n_new=n_new, head_dim=head_dim),
        grid_spec=grid_spec,
        out_shape=jax.ShapeDtypeStruct((bs, n_new, hd), F32),
        compiler_params=_params(("arbitrary",)),
    )(page_table, *([cache_kt] * npg), *([cache_vt] * npg), q, k_new, v_new)


def kernel(x_prompt, x_sample, state_ssm_re, state_ssm_im, cache_k, cache_v, page_table,
           ssm_norm, ssm_lambda_re, ssm_lambda_im, ssm_log_dt, ssm_b_re, ssm_b_im,
           ssm_c_re, ssm_c_im, ssm_d, ssm_w_glu, kv_norm, w_kv, k_norm,
           attn_norm, w_q, q_norm, w_o, mlp_norm, w_up, w_down):
    b_p, t_p, d_model = x_prompt.shape
    b_s, t_s, _ = x_sample.shape
    n_pool, page, n_heads, head_dim = cache_k.shape
    hd = n_heads * head_dim
    g_n, p_n = ssm_lambda_re.shape[1:]
    gp = g_n * p_n
    past_len = page_table.shape[1] * page
    n_p, n_s = t_p * b_p, t_s * b_s
    assert ssm_norm.shape[0] == 1 and attn_norm.shape[0] == 1, "one S5 layer then one MoBA layer"
    assert b_p % SUBLANES == 0 and b_s % SUBLANES == 0 and t_p % MOBA_BLOCK == 0
    assert g_n % SSM_CHUNK_GROUPS == 0 and hd % LANES == 0
    assert t_p % SSM_TIME_TILE == 0 and n_p % MLP_ROWS == 0 and n_s % MLP_ROWS == 0
    assert SAMPLE_KVQ_ROWS % t_s == 0 and n_s % SAMPLE_KVQ_ROWS == 0

    row = lambda w: w.reshape(1, -1)
    a_b, bmat, cmat = _ssm_weights(ssm_lambda_re[0], ssm_lambda_im[0], ssm_log_dt[0],
                                   ssm_b_re[0], ssm_b_im[0], ssm_c_re[0], ssm_c_im[0])
    wglu = ssm_w_glu[0].astype(BF16)
    wup = [w_up[layer].astype(BF16) for layer in range(2)]
    wdn = [w_down[layer].astype(BF16) for layer in range(2)]
    wkv = w_kv.astype(BF16)
    wq = w_q[0].astype(BF16)
    wo = w_o[0].astype(BF16)
    head_of = jnp.arange(hd) // head_dim
    e_mat = (head_of[:, None] == head_of[None, :]).astype(BF16)
    kn_t = row(jnp.tile(k_norm, n_heads))
    qn_t = row(jnp.tile(q_norm[0], n_heads))

    h0_p = jnp.zeros((b_p // SUBLANES, SUBLANES, 2 * gp), F32)
    h0_s = jnp.concatenate([state_ssm_re[0].reshape(b_s, gp), state_ssm_im[0].reshape(b_s, gp)],
                           axis=1).reshape(b_s // SUBLANES, SUBLANES, 2 * gp)
    ssm_args = (row(ssm_norm[0]), a_b, bmat, cmat, row(ssm_d[0]), wglu)
    hp, fin_p = _ssm_layer(x_prompt, h0_p, *ssm_args, tt=SSM_TIME_TILE)
    hs, fin_s = _ssm_layer(x_sample, h0_s, *ssm_args, tt=t_s)

    def split_state(fin, b_n):
        fin = fin.reshape(b_n, 2 * gp)
        return fin[:, :gp].reshape(1, b_n, g_n, p_n), fin[:, gp:].reshape(1, b_n, g_n, p_n)

    ssm_re_p, ssm_im_p = split_state(fin_p, b_p)
    ssm_re_s, ssm_im_s = split_state(fin_s, b_s)

    mlp0 = (row(mlp_norm[0]), wup[0], wdn[0])
    hp = _mlp(hp.reshape(n_p, d_model), *mlp0)
    hs = _mlp(hs.reshape(n_s, d_model), *mlp0)

    nb_p = t_p // MOBA_BLOCK
    cos_p, sin_p = _rope_cos_sin(jnp.arange(t_p, dtype=jnp.int32), head_dim)
    rope_pt = jnp.concatenate([cos_p.T, sin_p.T], axis=0)
    lanes_of = lambda g: jnp.broadcast_to(g[:, None], (head_dim, MOBA_BLOCK))
    k_p, v_p, kb_p, qt_p, vt_p, km_p = _kvq_prompt(
        hp, rope_pt, b_p, row(kv_norm), row(attn_norm[0]), wkv.T, wq.T,
        lanes_of(k_norm), lanes_of(q_norm[0] * (head_dim ** -0.5 * LOG2E)), head_dim)
    pos_s = past_len + jnp.arange(t_s, dtype=jnp.int32)
    rope_s = jnp.tile(_rope_table(pos_s, head_dim), (SAMPLE_KVQ_ROWS // t_s, 1))
    k_s, v_s, q_s = _kvq_sample(hs, rope_s, SAMPLE_KVQ_ROWS, row(kv_norm), row(attn_norm[0]),
                                wkv, wq, kn_t, qn_t, e_mat, head_dim)

    n_pairs = hd // LANES
    km_p = jnp.transpose(km_p.reshape(b_p, nb_p, n_pairs, LANES), (0, 2, 1, 3))
    km_p = jnp.pad(km_p, ((0, 0), (0, 0), (0, -nb_p % SUBLANES), (0, 0)))
    o_p = _attn_prompt(qt_p.reshape(b_p, n_pairs, LANES, t_p), kb_p,
                       vt_p.reshape(b_p, n_pairs, LANES, t_p), km_p, head_dim)
    cache_kt = jnp.transpose(cache_k, (0, 2, 3, 1)).reshape(n_pool, hd, page)
    cache_vt = jnp.transpose(cache_v, (0, 2, 3, 1)).reshape(n_pool, hd, page)
    o_s = _attn_sample(page_table, cache_kt, cache_vt, q_s.reshape(b_s, t_s, hd),
                       k_s.reshape(b_s, t_s, hd), v_s.reshape(b_s, t_s, hd), head_dim)

    mlp1 = (row(mlp_norm[1]), wup[1], wdn[1])
    y_p = _mlp(hp, *mlp1, o=o_p.reshape(n_p, hd), wo=wo)
    y_s = _mlp(hs, *mlp1, o=o_s.reshape(n_s, hd), wo=wo)

    to_bthd = lambda a: jnp.transpose(a.reshape(b_p, n_heads, head_dim, t_p), (0, 3, 1, 2))
    return (y_p.reshape(b_p, t_p, d_model), y_s.reshape(b_s, t_s, d_model),
            to_bthd(k_p), to_bthd(v_p),
            k_s.reshape(b_s, t_s, n_heads, head_dim), v_s.reshape(b_s, t_s, n_heads, head_dim),
            ssm_re_p, ssm_im_p, ssm_re_s, ssm_im_s)
```
